```python
import math
import jax
import jax.numpy as jnp
from jax import lax
import numpy as np

D_MODEL = 2048
BATCH = 2
SEQ = 8192
DEPTH = 1

CTX_LEN = 256
GRID_W = 64
N_DELTA_HEADS = 8
HEAD_DIM = 128
D_DELTA = N_DELTA_HEADS * HEAD_DIM
D_CONV = D_MODEL - D_DELTA
SHORT_CONV_W = 5
CONFORMER_CONV_W = 31
CHUNK = 64
D_FF = -(-(8 * D_MODEL) // (3 * 256)) * 256
EPS = 1e-6
V_END = 3 * D_DELTA
STATE_END = V_END + 4 * N_DELTA_HEADS
Z_END = STATE_END + D_DELTA
D_IN = Z_END + 2 * D_CONV

kernel_name = "hybrid_deltanet_conformer_dit_layer"


def rms_norm(x, w):
    xf = x.astype(jnp.float32)
    y = xf * lax.rsqrt(jnp.mean(xf * xf, axis=-1, keepdims=True) + EPS)
    return (y * w.astype(jnp.float32)).astype(x.dtype)


def layer_norm(x, w, b):
    xf = x.astype(jnp.float32)
    mu = jnp.mean(xf, axis=-1, keepdims=True)
    var = jnp.mean(jnp.square(xf - mu), axis=-1, keepdims=True)
    y = (xf - mu) * lax.rsqrt(var + EPS) * w.astype(jnp.float32) + b.astype(jnp.float32)
    return y.astype(x.dtype)


def l2_normalize(x):
    xf = x.astype(jnp.float32)
    return xf * lax.rsqrt(jnp.sum(xf * xf, axis=-1, keepdims=True) + EPS)


def modulate(h, shift, scale):
    return h * (1 + scale) + shift


def depthwise_conv(u, w):
    pad = w.shape[0] // 2
    return lax.conv_general_dilated(
        u, w[:, None, :].astype(u.dtype), window_strides=(1,), padding=((pad, pad),),
        dimension_numbers=("NWC", "WIO", "NWC"), feature_group_count=u.shape[-1])


def split_heads(t):
    b, l, _ = t.shape
    return t.reshape(b, l, N_DELTA_HEADS, HEAD_DIM).transpose(0, 2, 1, 3)


def gated_delta_rule(q, k, v, beta, g, state0):
    b, h, l, dk = q.shape
    dv = v.shape[-1]
    n = l // CHUNK
    f32 = jnp.float32
    q = (q.astype(f32) * dk ** -0.5).reshape(b, h, n, CHUNK, dk)
    k = k.astype(f32).reshape(b, h, n, CHUNK, dk)
    v = v.astype(f32).reshape(b, h, n, CHUNK, dv)
    beta = beta.astype(f32).reshape(b, h, n, CHUNK)
    gcum = jnp.cumsum(g.astype(f32).reshape(b, h, n, CHUNK), axis=-1)
    pos = jnp.arange(CHUNK)
    incl = pos[:, None] >= pos[None, :]
    strict = pos[:, None] > pos[None, :]
    decay = jnp.exp(jnp.where(incl, gcum[..., :, None] - gcum[..., None, :], -jnp.inf))
    kb = k * beta[..., None]
    m = jnp.where(strict, jnp.einsum("bhncd,bhnsd->bhncs", kb, k) * decay, 0.0)
    eye = jnp.eye(CHUNK, dtype=f32)
    t_inv = lax.linalg.triangular_solve(eye + m, jnp.broadcast_to(eye, m.shape),
                                        left_side=True, lower=True, unit_diagonal=True)
    u = jnp.einsum("bhncs,bhnse->bhnce", t_inv, v * beta[..., None])
    w = jnp.einsum("bhncs,bhnsd->bhncd", t_inv, kb * jnp.exp(gcum)[..., None])
    attn = jnp.where(incl, jnp.einsum("bhncd,bhnsd->bhncs", q, k) * decay, 0.0)
    q_dec = q * jnp.exp(gcum)[..., None]
    k_dec = k * jnp.exp(gcum[..., -1:] - gcum)[..., None]
    chunk_decay = jnp.exp(gcum[..., -1])

    def step(state, inp):
        u_c, w_c, a_c, qd_c, kd_c, cd_c = inp
        v_new = u_c - jnp.einsum("bhcd,bhde->bhce", w_c, state)
        o_c = jnp.einsum("bhcd,bhde->bhce", qd_c, state) + jnp.einsum("bhcs,bhse->bhce", a_c, v_new)
        state = state * cd_c[..., None, None] + jnp.einsum("bhcd,bhce->bhde", kd_c, v_new)
        return state, o_c

    xs = tuple(jnp.moveaxis(t, 2, 0) for t in (u, w, attn, q_dec, k_dec, chunk_decay))
    state, o = lax.scan(step, state0.astype(f32), xs)
    return jnp.moveaxis(o, 0, 2).reshape(b, h, l, dv), state


def bidir_gated_delta(q, k, v, beta, g, state0):
    o_f, s_f = gated_delta_rule(q, k, v, beta[0], g[0], state0[0])
    rev = lambda t: jnp.flip(t, axis=2)
    o_b, s_b = gated_delta_rule(rev(q), rev(k), rev(v), rev(beta[1]), rev(g[1]), state0[1])
    return o_f + rev(o_b), jnp.stack([s_f, s_b])


def delta_inputs(p, conv_w, a_log, dt_bias):
    f32 = jnp.float32
    qkv = jax.nn.silu(depthwise_conv(p[..., :V_END], conv_w))
    q, k, v = jnp.split(qkv, 3, axis=-1)
    q, k, v = l2_normalize(split_heads(q)), l2_normalize(split_heads(k)), split_heads(v)
    b, l, _ = p.shape
    ba = p[..., V_END:STATE_END].astype(f32).reshape(b, l, 4, N_DELTA_HEADS)
    beta = jax.nn.sigmoid(ba[:, :, :2])
    g = -jnp.exp(a_log.astype(f32)) * jax.nn.softplus(ba[:, :, 2:] + dt_bias.astype(f32))
    return q, k, v, beta.transpose(2, 0, 3, 1), g.transpose(2, 0, 3, 1)


def delta_output(o, z, norm_w):
    y = rms_norm(o, norm_w) * jax.nn.silu(split_heads(z).astype(jnp.float32))
    b, h, l, d = y.shape
    return y.transpose(0, 2, 1, 3).reshape(b, l, h * d).astype(z.dtype)


def conformer_branch(p_glu, dw, ln_w, ln_b, n_rows):
    a, gate = jnp.split(p_glu, 2, axis=-1)
    u = a * jax.nn.sigmoid(gate)
    b, l, ch = u.shape
    u = depthwise_conv(u.reshape(b * n_rows, l // n_rows, ch), dw).reshape(b, l, ch)
    return jax.nn.silu(layer_norm(u, ln_w, ln_b))


def swiglu(h, w_gate, w_up, w_down):
    return (jax.nn.silu(h @ w_gate) * (h @ w_up)) @ w_down


def setup_inputs(seed: int = 0) -> dict:
    key = jax.random.key(seed)
    ks = jax.random.split(key, 24)
    f32 = jnp.float32
    nrm = lambda k, shape, s: jax.random.normal(k, shape, f32) * s
    gain = lambda k, shape: 1.0 + 0.02 * jax.random.normal(k, shape, f32)
    dt = jnp.exp(jax.random.uniform(ks[10], (DEPTH, 2, N_DELTA_HEADS), f32, math.log(1e-3), math.log(1e-1)))
    return {
        "x": nrm(ks[0], (BATCH, SEQ, D_MODEL), 1.0),
        "c": nrm(ks[1], (BATCH, D_MODEL), 1.0),
        "ctx": nrm(ks[2], (BATCH, CTX_LEN, D_MODEL), 1.0),
        "c_ctx": nrm(ks[3], (D_MODEL,), 1.0),
        "w_mod": nrm(ks[4], (DEPTH, D_MODEL, 6 * D_MODEL), 0.5 * D_MODEL ** -0.5),
        "b_mod": nrm(ks[5], (DEPTH, 6 * D_MODEL), 0.02),
        "mix_norm_pre": gain(ks[6], (DEPTH, D_MODEL)),
        "mix_norm_post": gain(ks[7], (DEPTH, D_MODEL)),
        "w_in": nrm(ks[8], (DEPTH, D_MODEL, D_IN), D_MODEL ** -0.5),
        "qkv_conv": nrm(ks[9], (DEPTH, SHORT_CONV_W, 3 * D_DELTA), SHORT_CONV_W ** -0.5),
        "a_log": jnp.log(jax.random.uniform(ks[11], (DEPTH, 2, N_DELTA_HEADS), f32, 1.0, 16.0)),
        "dt_bias": dt + jnp.log(-jnp.expm1(-dt)),
        "delta_out_norm": gain(ks[12], (DEPTH, HEAD_DIM)),
        "conf_dw": nrm(ks[13], (DEPTH, CONFORMER_CONV_W, D_CONV), CONFORMER_CONV_W ** -0.5),
        "conf_ln_w": gain(ks[14], (DEPTH, D_CONV)),
        "conf_ln_b": nrm(ks[15], (DEPTH, D_CONV), 0.02),
        "w_out": nrm(ks[16], (DEPTH, D_MODEL, D_MODEL), D_MODEL ** -0.5),
        "ffn_norm_pre": gain(ks[17], (DEPTH, D_MODEL)),
        "ffn_norm_post": gain(ks[18], (DEPTH, D_MODEL)),
        "w_gate": nrm(ks[19], (DEPTH, D_MODEL, D_FF), D_MODEL ** -0.5),
        "w_up": nrm(ks[20], (DEPTH, D_MODEL, D_FF), D_MODEL ** -0.5),
        "w_down": nrm(ks[21], (DEPTH, D_FF, D_MODEL), D_FF ** -0.5),
    }


def reference(x, c, ctx, c_ctx, w_mod, b_mod, mix_norm_pre, mix_norm_post, w_in, qkv_conv,
              a_log, dt_bias, delta_out_norm, conf_dw, conf_ln_w, conf_ln_b, w_out,
              ffn_norm_pre, ffn_norm_post, w_gate, w_up, w_down):
    n_lat = x.shape[1]
    rows = n_lat // GRID_W
    for i in range(DEPTH):
        last = i == DEPTH - 1
        mod = (jax.nn.silu(c) @ w_mod[i] + b_mod[i])[:, None, :]
        mod_c = jax.nn.silu(c_ctx) @ w_mod[i] + b_mod[i]
        sh1, sc1, g1, sh2, sc2, g2 = jnp.split(mod, 6, axis=-1)
        sh1c, sc1c, g1c, sh2c, sc2c, g2c = jnp.split(mod_c, 6, axis=-1)

        h = modulate(rms_norm(x, mix_norm_pre[i]), sh1, sc1)
        hc = modulate(rms_norm(ctx, mix_norm_pre[i]), sh1c, sc1c)
        p = h @ w_in[i]
        pc = hc @ (w_in[i][:, :STATE_END] if last else w_in[i])

        qc, kc, vc, bc, gc = delta_inputs(pc, qkv_conv[i], a_log[i], dt_bias[i])
        zero_state = jnp.zeros((2, ctx.shape[0], N_DELTA_HEADS, HEAD_DIM, HEAD_DIM), jnp.float32)
        o_c, s_ctx = bidir_gated_delta(qc, kc, vc, bc, gc, zero_state)
        q, k, v, bl, gl = delta_inputs(p, qkv_conv[i], a_log[i], dt_bias[i])
        o_lat, _ = bidir_gated_delta(q, k, v, bl, gl, s_ctx)

        mix = jnp.concatenate([
            delta_output(o_lat, p[..., STATE_END:Z_END], delta_out_norm[i]),
            conformer_branch(p[..., Z_END:], conf_dw[i], conf_ln_w[i], conf_ln_b[i], rows),
        ], axis=-1)
        x = x + g1 * rms_norm(mix @ w_out[i], mix_norm_post[i])
        if not last:
            mix_c = jnp.concatenate([
                delta_output(o_c, pc[..., STATE_END:Z_END], delta_out_norm[i]),
                conformer_branch(pc[..., Z_END:], conf_dw[i], conf_ln_w[i], conf_ln_b[i], 1),
            ], axis=-1)
            ctx = ctx + g1c * rms_norm(mix_c @ w_out[i], mix_norm_post[i])

        hf = modulate(rms_norm(x, ffn_norm_pre[i]), sh2, sc2)
        x = x + g2 * rms_norm(swiglu(hf, w_gate[i], w_up[i], w_down[i]), ffn_norm_post[i])
        if not last:
            hfc = modulate(rms_norm(ctx, ffn_norm_pre[i]), sh2c, sc2c)
            ctx = ctx + g2c * rms_norm(swiglu(hfc, w_gate[i], w_up[i], w_down[i]), ffn_norm_post[i])
    return x
```

```python
import functools

import jax
import jax.numpy as jnp
from jax import lax
from jax.experimental import pallas as pl
from jax.experimental.pallas import tpu as pltpu

F32 = jnp.float32
BF16 = jnp.bfloat16
HIGHEST = lax.Precision.HIGHEST

D_MODEL = 2048
N_HEADS = 8
HEAD_DIM = 128
D_DELTA = N_HEADS * HEAD_DIM
D_CONV = D_MODEL - D_DELTA
GRID_W = 64
CHUNK = 64
SHORT_W = 5
CONF_W = 31
EPS = 1e-6
V_END = 3 * D_DELTA
STATE_END = V_END + 4 * N_HEADS
Z_END = STATE_END + D_DELTA
D_MAIN = V_END + D_DELTA + 2 * D_CONV
LANES = 128
VMEM_LIMIT = 56 * 1024 * 1024

NT_DIMS = (((1,), (1,)), ((), ()))
TN_DIMS = (((0,), (0,)), ((), ()))


def _mm(a, b, dims=None):
    a = a.astype(BF16)
    b = b.astype(BF16)
    if dims is None:
        return jnp.dot(a, b, preferred_element_type=F32)
    return lax.dot_general(a, b, dims, preferred_element_type=F32)


def _silu(x):
    return x * jax.nn.sigmoid(x)


def _rms(x, w):
    return x * lax.rsqrt(jnp.mean(x * x, axis=-1, keepdims=True) + EPS) * w


def _mod_kernel(c_ref, w_ref, b_ref, o_ref):
    s = _silu(c_ref[...])
    o_ref[...] = jnp.dot(s, w_ref[...], preferred_element_type=F32, precision=HIGHEST) + b_ref[...]


def _mod_call(cc, w_mod, b_mod):
    n = w_mod.shape[1]
    tn = 1024
    return pl.pallas_call(
        _mod_kernel,
        grid=(n // tn,),
        in_specs=[
            pl.BlockSpec((8, D_MODEL), lambda j: (0, 0)),
            pl.BlockSpec((D_MODEL, tn), lambda j: (0, j)),
            pl.BlockSpec((1, tn), lambda j: (0, j)),
        ],
        out_specs=pl.BlockSpec((8, tn), lambda j: (0, j)),
        out_shape=jax.ShapeDtypeStruct((8, n), F32),
        compiler_params=pltpu.CompilerParams(
            dimension_semantics=("arbitrary",), vmem_limit_bytes=VMEM_LIMIT),
        name="mod",
    )(cc, w_mod, b_mod)


def _inproj_kernel(x_ref, nw_ref, sh_ref, sc_ref, w_ref, wg_ref, o_ref, og_ref, h_ref, *,
                   tiles_per_mod_row, mod_row0):
    i = pl.program_id(0)
    j = pl.program_id(1)

    @pl.when(j == 0)
    def _():
        r = mod_row0 + i // tiles_per_mod_row
        sh = sh_ref[pl.ds(r, 1), :]
        sc = sc_ref[pl.ds(r, 1), :]
        hh = (_rms(x_ref[...], nw_ref[...]) * (1.0 + sc) + sh).astype(BF16)
        h_ref[...] = hh
        og_ref[...] = jnp.dot(hh, wg_ref[...], preferred_element_type=F32)

    o_ref[...] = jnp.dot(h_ref[...], w_ref[...], preferred_element_type=F32)


def _inproj_call(x2, norm_w, mod, w_main, w_g, *, n_cols, tm, tn, tiles_per_mod_row, mod_row0):
    m = x2.shape[0]
    kern = functools.partial(_inproj_kernel, tiles_per_mod_row=tiles_per_mod_row, mod_row0=mod_row0)
    return pl.pallas_call(
        kern,
        grid=(m // tm, n_cols // tn),
        in_specs=[
            pl.BlockSpec((tm, D_MODEL), lambda i, j: (i, 0)),
            pl.BlockSpec((1, D_MODEL), lambda i, j: (0, 0)),
            pl.BlockSpec((8, D_MODEL), lambda i, j: (0, 0)),
            pl.BlockSpec((8, D_MODEL), lambda i, j: (0, 1)),
            pl.BlockSpec((D_MODEL, tn), lambda i, j: (0, j)),
            pl.BlockSpec((D_MODEL, LANES), lambda i, j: (0, 0)),
        ],
        out_specs=[
            pl.BlockSpec((tm, tn), lambda i, j: (i, j)),
            pl.BlockSpec((tm, LANES), lambda i, j: (i, 0)),
        ],
        out_shape=[
            jax.ShapeDtypeStruct((m, n_cols), F32),
            jax.ShapeDtypeStruct((m, LANES), F32),
        ],
        scratch_shapes=[pltpu.VMEM((tm, D_MODEL), BF16)],
        compiler_params=pltpu.CompilerParams(
            dimension_semantics=("arbitrary", "arbitrary"), vmem_limit_bytes=VMEM_LIMIT),
        name="in_proj",
    )(x2, norm_w, mod, mod, w_main, w_g)


def _gatefeat_kernel(ba_ref, prm_ref, fc_ref, fr_ref, *, n_chunks):
    lane = lax.broadcasted_iota(jnp.int32, (CHUNK, LANES), 1)
    rr = lax.broadcasted_iota(jnp.int32, (CHUNK, CHUNK), 0)
    cc = lax.broadcasted_iota(jnp.int32, (CHUNK, CHUNK), 1)
    ltri = (rr >= cc).astype(F32)
    utri = (rr <= cc).astype(F32)
    r2 = lax.broadcasted_iota(jnp.int32, (LANES, LANES), 0)
    c2 = lax.broadcasted_iota(jnp.int32, (LANES, LANES), 1)
    eye = (r2 == c2).astype(F32)
    neg_a = -jnp.exp(prm_ref[0:1, :])
    dtb = prm_ref[1:2, :]

    def body(ci, carry):
        s0 = pl.multiple_of(ci * CHUNK, CHUNK)
        x = ba_ref[pl.ds(s0, CHUNK), :]
        beta = jax.nn.sigmoid(x)
        y = x + dtb
        softplus = jnp.maximum(y, 0.0) + jnp.log1p(jnp.exp(-jnp.abs(y)))
        g = neg_a * softplus
        pre = jnp.dot(ltri, g, preferred_element_type=F32, precision=HIGHEST)
        suf = jnp.dot(utri, g, preferred_element_type=F32, precision=HIGHEST)
        f = jnp.where(lane < 2 * N_HEADS, beta, jnp.where(lane < 3 * N_HEADS, pre, suf))
        fc_ref[pl.ds(s0, CHUNK), :] = f
        ft = lax.dot_general(eye, f, NT_DIMS, preferred_element_type=F32, precision=HIGHEST)
        fr_ref[ci] = ft[0:4 * N_HEADS, :]
        return carry

    lax.fori_loop(0, n_chunks, body, 0)


def _gatefeat_call(ba, prm):
    b, l, _ = ba.shape
    n_chunks = l // CHUNK
    return pl.pallas_call(
        functools.partial(_gatefeat_kernel, n_chunks=n_chunks),
        grid=(b,),
        in_specs=[
            pl.BlockSpec((None, l, LANES), lambda i: (i, 0, 0)),
            pl.BlockSpec((8, LANES), lambda i: (0, 0)),
        ],
        out_specs=[
            pl.BlockSpec((None, l, LANES), lambda i: (i, 0, 0)),
            pl.BlockSpec((None, n_chunks, 4 * N_HEADS, CHUNK), lambda i: (i, 0, 0, 0)),
        ],
        out_shape=[
            jax.ShapeDtypeStruct((b, l, LANES), F32),
            jax.ShapeDtypeStruct((b, n_chunks, 4 * N_HEADS, CHUNK), F32),
        ],
        compiler_params=pltpu.CompilerParams(
            dimension_semantics=("arbitrary",), vmem_limit_bytes=VMEM_LIMIT),
        name="gate_feat",
    )(ba, prm)


QKV_PAD = 8


def _qkvconv_kernel(x_ref, w_ref, o_ref, pad_ref, *, seq, tile):
    j = pl.program_id(1)
    zeros = jnp.zeros((QKV_PAD, LANES), F32)
    pad_ref[0:QKV_PAD, :] = zeros
    pad_ref[QKV_PAD + seq:2 * QKV_PAD + seq, :] = zeros
    pad_ref[QKV_PAD:QKV_PAD + seq, :] = x_ref[...]
    w = w_ref[...]
    is_qk = j < 2 * N_HEADS
    for t0 in range(0, seq, tile):
        acc = None
        for tap in range(SHORT_W):
            start = t0 + QKV_PAD - SHORT_W // 2 + tap
            term = pad_ref[start:start + tile, :] * w[tap:tap + 1, :]
            acc = term if acc is None else acc + term
        y = _silu(acc)
        inv = lax.rsqrt(jnp.sum(y * y, axis=-1, keepdims=True) + EPS)
        o_ref[t0:t0 + tile, :] = y * jnp.where(is_qk, inv, 1.0)


def _qkvconv_call(p, conv_w, *, tile):
    b, l, _ = p.shape
    n_blocks = V_END // LANES
    return pl.pallas_call(
        functools.partial(_qkvconv_kernel, seq=l, tile=tile),
        grid=(b, n_blocks),
        in_specs=[
            pl.BlockSpec((None, l, LANES), lambda i, j: (i, 0, j)),
            pl.BlockSpec((SHORT_W, LANES), lambda i, j: (0, j)),
        ],
        out_specs=pl.BlockSpec((None, l, LANES), lambda i, j: (i, 0, j)),
        out_shape=jax.ShapeDtypeStruct((b, l, V_END), F32),
        scratch_shapes=[pltpu.VMEM((l + 2 * QKV_PAD, LANES), F32)],
        compiler_params=pltpu.CompilerParams(
            dimension_semantics=("arbitrary", "arbitrary"), vmem_limit_bytes=VMEM_LIMIT),
        name="qkv_conv",
    )(p, conv_w)


def _delta_kernel(q_ref, k_ref, v_ref, z_ref, fc_ref, fr_ref,
                  qc_ref, kc_ref, vc_ref, fcc_ref, frc_ref, nw_ref,
                  o_ref, acc_ref, *, seq, ctx_len, out_tile):
    h = pl.program_id(1)
    n_chunks = seq // CHUNK
    n_ctx_chunks = ctx_len // CHUNK
    lane = lax.broadcasted_iota(jnp.int32, (CHUNK, LANES), 1)
    rr = lax.broadcasted_iota(jnp.int32, (CHUNK, CHUNK), 0)
    cc = lax.broadcasted_iota(jnp.int32, (CHUNK, CHUNK), 1)
    eye = (rr == cc).astype(F32)
    q_scale = HEAD_DIM ** -0.5

    def col(x, idx):
        return jnp.sum(jnp.where(lane == idx, x, 0.0), axis=-1, keepdims=True)

    def chunk_step(refs, c, state, forward):
        qr, kr, vr, fcr, frr = refs
        s0 = pl.multiple_of(c * CHUNK, CHUNK)
        q = qr[pl.ds(s0, CHUNK), :] * q_scale
        k = kr[pl.ds(s0, CHUNK), :]
        v = vr[pl.ds(s0, CHUNK), :]
        f = fcr[pl.ds(s0, CHUNK), :]
        d = 0 if forward else 1
        beta = col(f, d * N_HEADS + h)
        gc = col(f, (2 + d) * N_HEADS + h)
        gr = frr[c, pl.ds((2 + d) * N_HEADS + h, 1), :]
        g_end = gr[:, CHUNK - 1:CHUNK] if forward else gr[:, 0:1]
        incl = (rr >= cc) if forward else (rr <= cc)
        strict = (rr > cc) if forward else (rr < cc)
        decay = jnp.where(incl, jnp.exp(jnp.where(incl, gc - gr, 0.0)), 0.0)
        kb = k * beta
        gram = _mm(jnp.concatenate([q, kb], axis=0), k, NT_DIMS)
        attn = gram[:CHUNK] * decay
        m = jnp.where(strict, gram[CHUNK:] * decay, 0.0)
        pw = -m
        t_inv = eye + pw
        for _ in range(5):
            pw = _mm(pw, pw)
            t_inv = t_inv + _mm(t_inv, pw)
        eg = jnp.exp(gc)
        uw = _mm(t_inv, jnp.concatenate([v * beta, kb * eg], axis=1))
        u = uw[:, :HEAD_DIM]
        w = uw[:, HEAD_DIM:]
        ws = _mm(jnp.concatenate([w, q * eg], axis=0), state)
        v_new = u - ws[:CHUNK]
        o = ws[CHUNK:] + _mm(attn, v_new)
        k_dec = k * jnp.exp(g_end - gc)
        state = state * jnp.exp(g_end) + _mm(k_dec, v_new, TN_DIMS)
        return o, state

    ctx_refs = (qc_ref, kc_ref, vc_ref, fcc_ref, frc_ref)
    lat_refs = (q_ref, k_ref, v_ref, fc_ref, fr_ref)
    zero_state = jnp.zeros((HEAD_DIM, HEAD_DIM), F32)

    def ctx_body(c, carry):
        sf, sb = carry
        _, sf = chunk_step(ctx_refs, c, sf, True)
        _, sb = chunk_step(ctx_refs, n_ctx_chunks - 1 - c, sb, False)
        return sf, sb

    states = lax.fori_loop(0, n_ctx_chunks, ctx_body, (zero_state, zero_state))

    acc_ref[...] = jnp.zeros((seq, HEAD_DIM), F32)

    def lat_body(c, carry):
        sf, sb = carry
        o_f, sf = chunk_step(lat_refs, c, sf, True)
        r_f = pl.multiple_of(c * CHUNK, CHUNK)
        acc_ref[pl.ds(r_f, CHUNK), :] += o_f
        cb = n_chunks - 1 - c
        o_b, sb = chunk_step(lat_refs, cb, sb, False)
        r_b = pl.multiple_of(cb * CHUNK, CHUNK)
        acc_ref[pl.ds(r_b, CHUNK), :] += o_b
        return sf, sb

    lax.fori_loop(0, n_chunks, lat_body, states)

    nw = nw_ref[...]

    def out_body(t, carry):
        r0 = pl.multiple_of(t * out_tile, out_tile)
        y = _rms(acc_ref[pl.ds(r0, out_tile), :], nw) * _silu(z_ref[pl.ds(r0, out_tile), :])
        o_ref[pl.ds(r0, out_tile), :] = y.astype(o_ref.dtype)
        return carry

    lax.fori_loop(0, seq // out_tile, out_body, 0)


def _delta_call(qkv, p, fc, fr, qkv_c, fc_c, fr_c, norm_w):
    b, l, _ = qkv.shape
    lc = qkv_c.shape[1]
    nh = N_HEADS

    def colblk(rows, off):
        return pl.BlockSpec((None, rows, LANES), lambda i, h, off=off: (i, 0, off + h))

    def whole(arr):
        shp = arr.shape[1:]
        return pl.BlockSpec((None,) + shp, lambda i, h, n=len(shp): (i,) + (0,) * n)

    return pl.pallas_call(
        functools.partial(_delta_kernel, seq=l, ctx_len=lc, out_tile=512),
        grid=(b, nh),
        in_specs=[
            colblk(l, 0), colblk(l, nh), colblk(l, 2 * nh),
            colblk(l, 3 * nh),
            whole(fc), whole(fr),
            colblk(lc, 0), colblk(lc, nh), colblk(lc, 2 * nh),
            whole(fc_c), whole(fr_c),
            pl.BlockSpec((1, HEAD_DIM), lambda i, h: (0, 0)),
        ],
        out_specs=pl.BlockSpec((None, l, LANES), lambda i, h: (i, 0, h)),
        out_shape=jax.ShapeDtypeStruct((b, l, D_DELTA), BF16),
        scratch_shapes=[pltpu.VMEM((l, HEAD_DIM), F32)],
        compiler_params=pltpu.CompilerParams(
            dimension_semantics=("arbitrary", "arbitrary"), vmem_limit_bytes=VMEM_LIMIT),
        name="delta",
    )(qkv, qkv, qkv, p, fc, fr, qkv_c, qkv_c, qkv_c, fc_c, fr_c, norm_w)


CONF_PAD = 16
CONF_CCHUNK = 256


def _conformer_kernel(a_ref, g_ref, dw_ref, lnw_ref, lnb_ref, o_ref, pad_ref, cv_ref, *, n_seg):
    zeros = jnp.zeros((CONF_PAD, D_CONV), F32)
    pad_ref[0:CONF_PAD, :] = zeros
    pad_ref[CONF_PAD + GRID_W:2 * CONF_PAD + GRID_W, :] = zeros
    lnw = lnw_ref[...]
    lnb = lnb_ref[...]

    def seg(s, carry):
        r0 = pl.multiple_of(s * GRID_W, GRID_W)
        pad_ref[CONF_PAD:CONF_PAD + GRID_W, :] = (
            a_ref[pl.ds(r0, GRID_W), :] * jax.nn.sigmoid(g_ref[pl.ds(r0, GRID_W), :]))
        for c0 in range(0, D_CONV, CONF_CCHUNK):
            acc = None
            for tap in range(CONF_W):
                start = CONF_PAD - CONF_W // 2 + tap
                term = (pad_ref[start:start + GRID_W, c0:c0 + CONF_CCHUNK]
                        * dw_ref[tap:tap + 1, c0:c0 + CONF_CCHUNK])
                acc = term if acc is None else acc + term
            cv_ref[:, c0:c0 + CONF_CCHUNK] = acc
        u = cv_ref[...]
        mu = jnp.mean(u, axis=-1, keepdims=True)
        uc = u - mu
        var = jnp.mean(uc * uc, axis=-1, keepdims=True)
        y = uc * lax.rsqrt(var + EPS) * lnw + lnb
        o_ref[pl.ds(r0, GRID_W), :] = _silu(y).astype(o_ref.dtype)
        return carry

    lax.fori_loop(0, n_seg, seg, 0)


def _conformer_call(p2, dw, lnw, lnb, *, tt):
    m = p2.shape[0]
    a_blk = (V_END + D_DELTA) // D_CONV
    return pl.pallas_call(
        functools.partial(_conformer_kernel, n_seg=tt // GRID_W),
        grid=(m // tt,),
        in_specs=[
            pl.BlockSpec((tt, D_CONV), lambda i: (i, a_blk)),
            pl.BlockSpec((tt, D_CONV), lambda i: (i, a_blk + 1)),
            pl.BlockSpec((CONF_W, D_CONV), lambda i: (0, 0)),
            pl.BlockSpec((1, D_CONV), lambda i: (0, 0)),
            pl.BlockSpec((1, D_CONV), lambda i: (0, 0)),
        ],
        out_specs=pl.BlockSpec((tt, D_CONV), lambda i: (i, 0)),
        out_shape=jax.ShapeDtypeStruct((m, D_CONV), BF16),
        scratch_shapes=[
            pltpu.VMEM((GRID_W + 2 * CONF_PAD, D_CONV), F32),
            pltpu.VMEM((GRID_W, D_CONV), F32),
        ],
        compiler_params=pltpu.CompilerParams(
            dimension_semantics=("arbitrary",), vmem_limit_bytes=VMEM_LIMIT),
        name="conformer",
    )(p2, p2, dw, lnw, lnb)


def _outproj_kernel(md_ref, mc_ref, w1_ref, w2_ref, x_ref, g_ref, nw_ref, o_ref, *, tiles_per_batch):
    b = pl.program_id(0) // tiles_per_batch
    y = (jnp.dot(md_ref[...], w1_ref[...], preferred_element_type=F32)
         + jnp.dot(mc_ref[...], w2_ref[...], preferred_element_type=F32))
    g = g_ref[pl.ds(b, 1), :]
    o_ref[...] = x_ref[...] + g * _rms(y, nw_ref[...])


def _outproj_call(md, mc, w_out, x2, mod, norm_w, *, tm, tiles_per_batch):
    m = x2.shape[0]
    return pl.pallas_call(
        functools.partial(_outproj_kernel, tiles_per_batch=tiles_per_batch),
        grid=(m // tm,),
        in_specs=[
            pl.BlockSpec((tm, D_DELTA), lambda i: (i, 0)),
            pl.BlockSpec((tm, D_CONV), lambda i: (i, 0)),
            pl.BlockSpec((D_DELTA, D_MODEL), lambda i: (0, 0)),
            pl.BlockSpec((D_CONV, D_MODEL), lambda i: (1, 0)),
            pl.BlockSpec((tm, D_MODEL), lambda i: (i, 0)),
            pl.BlockSpec((8, D_MODEL), lambda i: (0, 2)),
            pl.BlockSpec((1, D_MODEL), lambda i: (0, 0)),
        ],
        out_specs=pl.BlockSpec((tm, D_MODEL), lambda i: (i, 0)),
        out_shape=jax.ShapeDtypeStruct((m, D_MODEL), F32),
        compiler_params=pltpu.CompilerParams(
            dimension_semantics=("arbitrary",), vmem_limit_bytes=VMEM_LIMIT),
        name="out_proj",
    )(md, mc, w_out, w_out, x2, mod, norm_w)


def _ffn_kernel(x_ref, nw_ref, sh_ref, sc_ref, g_ref, pw_ref, wg_ref, wu_ref, wd_ref, o_ref, h_ref, *,
                tiles_per_batch, n_k):
    i = pl.program_id(0)
    k = pl.program_id(1)
    b = i // tiles_per_batch

    @pl.when(k == 0)
    def _():
        sh = sh_ref[pl.ds(b, 1), :]
        sc = sc_ref[pl.ds(b, 1), :]
        h_ref[...] = (_rms(x_ref[...], nw_ref[...]) * (1.0 + sc) + sh).astype(BF16)

    hh = h_ref[...]
    gate = jnp.dot(hh, wg_ref[...], preferred_element_type=F32)
    up = jnp.dot(hh, wu_ref[...], preferred_element_type=F32)
    act = (_silu(gate) * up).astype(BF16)
    part = jnp.dot(act, wd_ref[...], preferred_element_type=F32)

    @pl.when(k == 0)
    def _():
        o_ref[...] = part

    @pl.when(k > 0)
    def _():
        o_ref[...] += part

    @pl.when(k == n_k - 1)
    def _():
        g = g_ref[pl.ds(b, 1), :]
        o_ref[...] = x_ref[...] + g * _rms(o_ref[...], pw_ref[...])


def _ffn_call(x2, mod, norm_pre, norm_post, wg, wu, wd, *, tm, tf, tiles_per_batch):
    m = x2.shape[0]
    dff = wg.shape[1]
    n_k = dff // tf
    return pl.pallas_call(
        functools.partial(_ffn_kernel, tiles_per_batch=tiles_per_batch, n_k=n_k),
        grid=(m // tm, n_k),
        in_specs=[
            pl.BlockSpec((tm, D_MODEL), lambda i, k: (i, 0)),
            pl.BlockSpec((1, D_MODEL), lambda i, k: (0, 0)),
            pl.BlockSpec((8, D_MODEL), lambda i, k: (0, 3)),
            pl.BlockSpec((8, D_MODEL), lambda i, k: (0, 4)),
            pl.BlockSpec((8, D_MODEL), lambda i, k: (0, 5)),
            pl.BlockSpec((1, D_MODEL), lambda i, k: (0, 0)),
            pl.BlockSpec((D_MODEL, tf), lambda i, k: (0, k)),
            pl.BlockSpec((D_MODEL, tf), lambda i, k: (0, k)),
            pl.BlockSpec((tf, D_MODEL), lambda i, k: (k, 0)),
        ],
        out_specs=pl.BlockSpec((tm, D_MODEL), lambda i, k: (i, 0)),
        out_shape=jax.ShapeDtypeStruct((m, D_MODEL), F32),
        scratch_shapes=[pltpu.VMEM((tm, D_MODEL), BF16)],
        compiler_params=pltpu.CompilerParams(
            dimension_semantics=("arbitrary", "arbitrary"), vmem_limit_bytes=VMEM_LIMIT),
        name="ffn",
    )(x2, norm_pre, mod, mod, mod, norm_post, wg, wu, wd)


@jax.jit
def _forward(x, c, ctx, c_ctx, w_mod, b_mod, mix_norm_pre, mix_norm_post, w_in, qkv_conv,
             a_log, dt_bias, delta_out_norm, conf_dw, conf_ln_w, conf_ln_b, w_out,
             ffn_norm_pre, ffn_norm_post, w_gate, w_up, w_down):
    bsz, seq, d = x.shape
    ctx_len = ctx.shape[1]
    assert d == D_MODEL and w_mod.shape[0] == 1, "single-layer kernel"
    assert seq % 1024 == 0 and ctx_len % CHUNK == 0 and bsz + 1 <= 8

    cc = jnp.zeros((8, D_MODEL), F32).at[:bsz].set(c).at[bsz].set(c_ctx)
    mod = _mod_call(cc, w_mod[0], b_mod[0][None, :])

    w_in0 = w_in[0]
    w_main = jnp.concatenate([w_in0[:, :V_END], w_in0[:, STATE_END:]], axis=1).astype(BF16)
    w_g = jnp.zeros((D_MODEL, LANES), BF16).at[:, :4 * N_HEADS].set(
        w_in0[:, V_END:STATE_END].astype(BF16))
    prm = (jnp.zeros((8, LANES), F32)
           .at[0, 2 * N_HEADS:4 * N_HEADS].set(a_log[0].reshape(-1))
           .at[1, 2 * N_HEADS:4 * N_HEADS].set(dt_bias[0].reshape(-1)))

    x2 = x.reshape(bsz * seq, D_MODEL)
    ctx2 = ctx.reshape(bsz * ctx_len, D_MODEL)
    tm = 1024
    p, ba = _inproj_call(x2, mix_norm_pre, mod, w_main, w_g, n_cols=D_MAIN, tm=tm, tn=768,
                         tiles_per_mod_row=seq // tm, mod_row0=0)
    pc, bac = _inproj_call(ctx2, mix_norm_pre, mod, w_main, w_g, n_cols=V_END, tm=bsz * ctx_len, tn=768,
                           tiles_per_mod_row=1, mod_row0=bsz)

    p3 = p.reshape(bsz, seq, D_MAIN)
    pc3 = pc.reshape(bsz, ctx_len, V_END)
    fc, fr = _gatefeat_call(ba.reshape(bsz, seq, LANES), prm)
    fc_c, fr_c = _gatefeat_call(bac.reshape(bsz, ctx_len, LANES), prm)
    qkv = _qkvconv_call(p3, qkv_conv[0], tile=256)
    qkv_c = _qkvconv_call(pc3, qkv_conv[0], tile=256)

    mix_d = _delta_call(qkv, p3, fc, fr, qkv_c, fc_c, fr_c, delta_out_norm)
    mix_c = _conformer_call(p, conf_dw[0], conf_ln_w, conf_ln_b, tt=512)

    x1 = _outproj_call(mix_d.reshape(bsz * seq, D_DELTA), mix_c, w_out[0].astype(BF16), x2, mod,
                       mix_norm_post, tm=512, tiles_per_batch=seq // 512)
    out = _ffn_call(x1, mod, ffn_norm_pre, ffn_norm_post, w_gate[0].astype(BF16), w_up[0].astype(BF16),
                    w_down[0].astype(BF16), tm=512, tf=512, tiles_per_batch=seq // 512)
    return out.reshape(bsz, seq, D_MODEL)


def kernel(x, c, ctx, c_ctx, w_mod, b_mod, mix_norm_pre, mix_norm_post, w_in, qkv_conv, a_log, dt_bias,
           delta_out_norm, conf_dw, conf_ln_w, conf_ln_b, w_out, ffn_norm_pre, ffn_norm_post,
           w_gate, w_up, w_down):
    return _forward(x, c, ctx, c_ctx, w_mod, b_mod, mix_norm_pre, mix_norm_post, w_in, qkv_conv,
                    a_log, dt_bias, delta_out_norm, conf_dw, conf_ln_w, conf_ln_b, w_out,
                    ffn_norm_pre, ffn_norm_post, w_gate, w_up, w_down)
```

```python
import functools

import jax
import jax.numpy as jnp
from jax import lax
from jax.experimental import pallas as pl
from jax.experimental.pallas import tpu as pltpu

F32 = jnp.float32
BF16 = jnp.bfloat16
HIGHEST = lax.Precision.HIGHEST

D_MODEL = 2048
N_HEADS = 8
HEAD_DIM = 128
D_DELTA = N_HEADS * HEAD_DIM
D_CONV = D_MODEL - D_DELTA
GRID_W = 64
CHUNK = 64
SHORT_W = 5
CONF_W = 31
EPS = 1e-6
V_END = 3 * D_DELTA
STATE_END = V_END + 4 * N_HEADS
Z_END = STATE_END + D_DELTA
D_MAIN = V_END + D_DELTA + 2 * D_CONV
N_FEAT = 4 * N_HEADS
LANES = 128
VMEM_LIMIT = 56 * 1024 * 1024

NT_DIMS = (((1,), (1,)), ((), ()))
TN_DIMS = (((0,), (0,)), ((), ()))


def _mm(a, b, dims=None):
    a = a.astype(BF16)
    b = b.astype(BF16)
    if dims is None:
        return jnp.dot(a, b, preferred_element_type=F32)
    return lax.dot_general(a, b, dims, preferred_element_type=F32)


def _silu(x):
    return x * jax.nn.sigmoid(x)


def _rms(x, w):
    return x * lax.rsqrt(jnp.mean(x * x, axis=-1, keepdims=True) + EPS) * w


def _mod_kernel(c_ref, w_ref, b_ref, o_ref):
    s = _silu(c_ref[...])
    o_ref[...] = jnp.dot(s, w_ref[...], preferred_element_type=F32, precision=HIGHEST) + b_ref[...]


def _mod_call(cc, w_mod, b_mod):
    n = w_mod.shape[1]
    tn = 1024
    return pl.pallas_call(
        _mod_kernel,
        grid=(n // tn,),
        in_specs=[
            pl.BlockSpec((8, D_MODEL), lambda j: (0, 0)),
            pl.BlockSpec((D_MODEL, tn), lambda j: (0, j)),
            pl.BlockSpec((1, tn), lambda j: (0, j)),
        ],
        out_specs=pl.BlockSpec((8, tn), lambda j: (0, j)),
        out_shape=jax.ShapeDtypeStruct((8, n), F32),
        compiler_params=pltpu.CompilerParams(
            dimension_semantics=("arbitrary",), vmem_limit_bytes=VMEM_LIMIT),
        name="mod",
    )(cc, w_mod, b_mod)


def _inproj_kernel(x_ref, nw_ref, sh_ref, sc_ref, w_ref, wg_ref, o_ref, og_ref, h_ref, *,
                   tiles_per_mod_row, mod_row0):
    i = pl.program_id(0)
    j = pl.program_id(1)

    @pl.when(j == 0)
    def _():
        r = mod_row0 + i // tiles_per_mod_row
        sh = sh_ref[pl.ds(r, 1), :]
        sc = sc_ref[pl.ds(r, 1), :]
        hh = (_rms(x_ref[...], nw_ref[...]) * (1.0 + sc) + sh).astype(BF16)
        h_ref[...] = hh
        og_ref[...] = jnp.dot(hh, wg_ref[...], preferred_element_type=F32)

    o_ref[...] = jnp.dot(h_ref[...], w_ref[...], preferred_element_type=F32)


def _inproj_call(x2, norm_w, mod, w_main, w_g, *, n_cols, tm, tn, tiles_per_mod_row, mod_row0):
    m = x2.shape[0]
    kern = functools.partial(_inproj_kernel, tiles_per_mod_row=tiles_per_mod_row, mod_row0=mod_row0)
    return pl.pallas_call(
        kern,
        grid=(m // tm, n_cols // tn),
        in_specs=[
            pl.BlockSpec((tm, D_MODEL), lambda i, j: (i, 0)),
            pl.BlockSpec((1, D_MODEL), lambda i, j: (0, 0)),
            pl.BlockSpec((8, D_MODEL), lambda i, j: (0, 0)),
            pl.BlockSpec((8, D_MODEL), lambda i, j: (0, 1)),
            pl.BlockSpec((D_MODEL, tn), lambda i, j: (0, j)),
            pl.BlockSpec((D_MODEL, LANES), lambda i, j: (0, 0)),
        ],
        out_specs=[
            pl.BlockSpec((tm, tn), lambda i, j: (i, j)),
            pl.BlockSpec((tm, LANES), lambda i, j: (i, 0)),
        ],
        out_shape=[
            jax.ShapeDtypeStruct((m, n_cols), F32),
            jax.ShapeDtypeStruct((m, LANES), F32),
        ],
        scratch_shapes=[pltpu.VMEM((tm, D_MODEL), BF16)],
        compiler_params=pltpu.CompilerParams(
            dimension_semantics=("arbitrary", "arbitrary"), vmem_limit_bytes=VMEM_LIMIT),
        name="in_proj",
    )(x2, norm_w, mod, mod, w_main, w_g)


def _gatefeat_kernel(ba_ref, prm_ref, fc_ref, fr_ref, *, n_chunks):
    lane = lax.broadcasted_iota(jnp.int32, (CHUNK, LANES), 1)
    rr = lax.broadcasted_iota(jnp.int32, (CHUNK, CHUNK), 0)
    cc = lax.broadcasted_iota(jnp.int32, (CHUNK, CHUNK), 1)
    ltri = (rr >= cc).astype(F32)
    utri = (rr <= cc).astype(F32)
    r2 = lax.broadcasted_iota(jnp.int32, (LANES, LANES), 0)
    c2 = lax.broadcasted_iota(jnp.int32, (LANES, LANES), 1)
    eye = (r2 == c2).astype(F32)
    neg_a = -jnp.exp(prm_ref[0:1, :])
    dtb = prm_ref[1:2, :]

    def body(ci, carry):
        s0 = pl.multiple_of(ci * CHUNK, CHUNK)
        x = ba_ref[pl.ds(s0, CHUNK), :]
        beta = jax.nn.sigmoid(x)
        y = x + dtb
        softplus = jnp.maximum(y, 0.0) + jnp.log1p(jnp.exp(-jnp.abs(y)))
        g = neg_a * softplus
        pre = jnp.dot(ltri, g, preferred_element_type=F32, precision=HIGHEST)
        suf = jnp.dot(utri, g, preferred_element_type=F32, precision=HIGHEST)
        f = jnp.where(lane < 2 * N_HEADS, beta, jnp.where(lane < 3 * N_HEADS, pre, suf))
        fc_ref[pl.ds(s0, CHUNK), :] = f
        ft = lax.dot_general(eye, f, NT_DIMS, preferred_element_type=F32, precision=HIGHEST)
        fr_ref[ci] = ft[0:N_FEAT, :]
        return carry

    lax.fori_loop(0, n_chunks, body, 0)


def _gatefeat_call(ba, prm):
    b, l, _ = ba.shape
    n_chunks = l // CHUNK
    return pl.pallas_call(
        functools.partial(_gatefeat_kernel, n_chunks=n_chunks),
        grid=(b,),
        in_specs=[
            pl.BlockSpec((None, l, LANES), lambda i: (i, 0, 0)),
            pl.BlockSpec((8, LANES), lambda i: (0, 0)),
        ],
        out_specs=[
            pl.BlockSpec((None, l, LANES), lambda i: (i, 0, 0)),
            pl.BlockSpec((None, n_chunks, N_FEAT, CHUNK), lambda i: (i, 0, 0, 0)),
        ],
        out_shape=[
            jax.ShapeDtypeStruct((b, l, LANES), F32),
            jax.ShapeDtypeStruct((b, n_chunks, N_FEAT, CHUNK), F32),
        ],
        compiler_params=pltpu.CompilerParams(
            dimension_semantics=("arbitrary",), vmem_limit_bytes=VMEM_LIMIT),
        name="gate_feat",
    )(ba, prm)


QKV_PAD = 8


def _qkvconv_kernel(x_ref, w_ref, o_ref, pad_ref, *, seq, tile):
    j = pl.program_id(1)
    zeros = jnp.zeros((QKV_PAD, LANES), F32)
    pad_ref[0:QKV_PAD, :] = zeros
    pad_ref[QKV_PAD + seq:2 * QKV_PAD + seq, :] = zeros
    pad_ref[QKV_PAD:QKV_PAD + seq, :] = x_ref[...]
    w = w_ref[...]
    is_qk = j < 2 * N_HEADS
    for t0 in range(0, seq, tile):
        acc = None
        for tap in range(SHORT_W):
            start = t0 + QKV_PAD - SHORT_W // 2 + tap
            term = pad_ref[start:start + tile, :] * w[tap:tap + 1, :]
            acc = term if acc is None else acc + term
        y = _silu(acc)
        inv = lax.rsqrt(jnp.sum(y * y, axis=-1, keepdims=True) + EPS)
        o_ref[t0:t0 + tile, :] = y * jnp.where(is_qk, inv, 1.0)


def _qkvconv_call(p, conv_w, *, tile):
    b, l, _ = p.shape
    n_blocks = V_END // LANES
    return pl.pallas_call(
        functools.partial(_qkvconv_kernel, seq=l, tile=tile),
        grid=(b, n_blocks),
        in_specs=[
            pl.BlockSpec((None, l, LANES), lambda i, j: (i, 0, j)),
            pl.BlockSpec((SHORT_W, LANES), lambda i, j: (0, j)),
        ],
        out_specs=pl.BlockSpec((None, l, LANES), lambda i, j: (i, 0, j)),
        out_shape=jax.ShapeDtypeStruct((b, l, V_END), F32),
        scratch_shapes=[pltpu.VMEM((l + 2 * QKV_PAD, LANES), F32)],
        compiler_params=pltpu.CompilerParams(
            dimension_semantics=("arbitrary", "arbitrary"), vmem_limit_bytes=VMEM_LIMIT),
        name="qkv_conv",
    )(p, conv_w)


def _delta_kernel(xf_ref, xb_ref, fcf_ref, fcb_ref, frf_ref, frb_ref, init_ref,
                  of_ref, ob_ref, st_ref, *, n_pos):
    @pl.when(pl.program_id(1) == 0)
    def _():
        st_ref[...] = init_ref[...]

    rr = lax.broadcasted_iota(jnp.int32, (CHUNK, CHUNK), 0)
    cc = lax.broadcasted_iota(jnp.int32, (CHUNK, CHUNK), 1)
    eye = (rr == cc).astype(F32)
    q_scale = HEAD_DIM ** -0.5

    class Chain:
        def __init__(self, x_ref, fc_ref, fr_ref, o_ref, c, h, forward):
            self.x_ref, self.fc_ref, self.fr_ref, self.o_ref = x_ref, fc_ref, fr_ref, o_ref
            self.c, self.h, self.forward = c, h, forward
            self.d = 0 if forward else 1
            self.s0 = pl.multiple_of(c * CHUNK, CHUNK)
            self.lo = h * HEAD_DIM

        def tile(self, which):
            off = which * D_DELTA + self.lo
            return self.x_ref[pl.ds(self.s0, CHUNK), off:off + HEAD_DIM]

        def feats(self):
            f = self.fc_ref[pl.ds(self.s0, CHUNK), :]
            ib = self.d * N_HEADS + self.h
            ig = (2 + self.d) * N_HEADS + self.h
            beta = f[:, ib:ib + 1]
            gc = f[:, ig:ig + 1]
            gr = self.fr_ref[self.c, ig:ig + 1, :]
            g_end = gr[:, CHUNK - 1:CHUNK] if self.forward else gr[:, 0:1]
            return beta, gc, gr, g_end

    def stage_gram(ch):
        beta, gc, gr, _ = ch.feats()
        k = ch.tile(1)
        ch.gram = _mm(jnp.concatenate([ch.tile(0) * q_scale, k * beta], axis=0), k, NT_DIMS)
        incl = (rr >= cc) if ch.forward else (rr <= cc)
        ch.decay = jnp.where(incl, jnp.exp(jnp.where(incl, gc - gr, 0.0)), 0.0)

    def stage_power0(ch):
        strict = (rr > cc) if ch.forward else (rr < cc)
        ch.attn = (ch.gram[:CHUNK] * ch.decay).astype(BF16)
        neg = -jnp.where(strict, ch.gram[CHUNK:] * ch.decay, 0.0)
        ch.neg = neg.astype(BF16)
        ch.t_inv = eye + neg
        ch.pw = _mm(ch.neg, ch.neg)
        del ch.gram, ch.decay

    def stage_power(ch):
        both = _mm(jnp.concatenate([ch.pw, ch.t_inv], axis=0), ch.pw)
        ch.t_inv = ch.t_inv + both[CHUNK:]
        ch.pw = both[:CHUNK]

    def stage_power_last(ch):
        ch.t_inv = (ch.t_inv + _mm(ch.t_inv, ch.pw)).astype(BF16)
        del ch.pw

    def stage_newton_residual(ch):
        ch.err = eye - ch.t_inv.astype(F32) + _mm(ch.neg, ch.t_inv)
        del ch.neg

    def stage_newton_apply(ch):
        ch.t_inv = (ch.t_inv.astype(F32) + _mm(ch.t_inv, ch.err)).astype(BF16)
        del ch.err

    def stage_predict(ch):
        beta, gc, _, _ = ch.feats()
        eg = jnp.exp(gc)
        kbg = ch.tile(1) * (beta * eg)
        qd = ch.tile(0) * (q_scale * eg)
        ps = _mm(jnp.concatenate([kbg, qd], axis=0), st_ref[ch.d, ch.h])
        ch.resid = ch.tile(2) * beta - ps[:CHUNK]
        ch.qs = ps[CHUNK:]

    def stage_solve(ch):
        ch.v_new = _mm(ch.t_inv, ch.resid).astype(BF16)
        del ch.resid, ch.t_inv

    def stage_update(ch):
        _, gc, _, g_end = ch.feats()
        ch.o_ref[pl.ds(ch.s0, CHUNK), ch.lo:ch.lo + HEAD_DIM] = ch.qs + _mm(ch.attn, ch.v_new)
        k_dec = ch.tile(1) * jnp.exp(g_end - gc)
        st_ref[ch.d, ch.h] = st_ref[ch.d, ch.h] * jnp.exp(g_end) + _mm(k_dec, ch.v_new, TN_DIMS)

    stages = ([stage_gram, stage_power0] + [stage_power] * 4
              + [stage_power_last, stage_newton_residual, stage_newton_apply,
                 stage_predict, stage_solve, stage_update])

    def body(c, carry):
        cb = n_pos - 1 - c
        chains = []
        for h in range(N_HEADS):
            chains.append(Chain(xf_ref, fcf_ref, frf_ref, of_ref, c, h, True))
            chains.append(Chain(xb_ref, fcb_ref, frb_ref, ob_ref, cb, h, False))
        for stage in stages:
            for ch in chains:
                stage(ch)
        return carry

    lax.fori_loop(0, n_pos, body, 0)


def _delta_call(qkv, fc, fr, init, *, lb):
    b, l, _ = qkv.shape
    nb = l // lb
    n_pos = lb // CHUNK
    fwd3 = lambda i, j: (i, j, 0)
    bwd3 = lambda i, j: (i, nb - 1 - j, 0)
    st_spec = pl.BlockSpec((None, 2, N_HEADS, HEAD_DIM, HEAD_DIM), lambda i, j: (i, 0, 0, 0, 0))
    return pl.pallas_call(
        functools.partial(_delta_kernel, n_pos=n_pos),
        grid=(b, nb),
        in_specs=[
            pl.BlockSpec((None, lb, V_END), fwd3),
            pl.BlockSpec((None, lb, V_END), bwd3),
            pl.BlockSpec((None, lb, LANES), fwd3),
            pl.BlockSpec((None, lb, LANES), bwd3),
            pl.BlockSpec((None, n_pos, N_FEAT, CHUNK), lambda i, j: (i, j, 0, 0)),
            pl.BlockSpec((None, n_pos, N_FEAT, CHUNK), lambda i, j: (i, nb - 1 - j, 0, 0)),
            st_spec,
        ],
        out_specs=[
            pl.BlockSpec((None, lb, D_DELTA), fwd3),
            pl.BlockSpec((None, lb, D_DELTA), bwd3),
            st_spec,
        ],
        out_shape=[
            jax.ShapeDtypeStruct((b, l, D_DELTA), F32),
            jax.ShapeDtypeStruct((b, l, D_DELTA), F32),
            jax.ShapeDtypeStruct((b, 2, N_HEADS, HEAD_DIM, HEAD_DIM), F32),
        ],
        compiler_params=pltpu.CompilerParams(
            dimension_semantics=("arbitrary", "arbitrary"), vmem_limit_bytes=VMEM_LIMIT),
        name="delta",
    )(qkv, qkv, fc, fc, fr, fr, init)


CONF_PAD = 16
CONF_CCHUNK = 256


def _conformer_kernel(a_ref, g_ref, dw_ref, lnw_ref, lnb_ref, o_ref, pad_ref, cv_ref, *, n_seg):
    zeros = jnp.zeros((CONF_PAD, D_CONV), F32)
    pad_ref[0:CONF_PAD, :] = zeros
    pad_ref[CONF_PAD + GRID_W:2 * CONF_PAD + GRID_W, :] = zeros
    lnw = lnw_ref[...]
    lnb = lnb_ref[...]

    def seg(s, carry):
        r0 = pl.multiple_of(s * GRID_W, GRID_W)
        pad_ref[CONF_PAD:CONF_PAD + GRID_W, :] = (
            a_ref[pl.ds(r0, GRID_W), :] * jax.nn.sigmoid(g_ref[pl.ds(r0, GRID_W), :]))
        for c0 in range(0, D_CONV, CONF_CCHUNK):
            acc = None
            for tap in range(CONF_W):
                start = CONF_PAD - CONF_W // 2 + tap
                term = (pad_ref[start:start + GRID_W, c0:c0 + CONF_CCHUNK]
                        * dw_ref[tap:tap + 1, c0:c0 + CONF_CCHUNK])
                acc = term if acc is None else acc + term
            cv_ref[:, c0:c0 + CONF_CCHUNK] = acc
        u = cv_ref[...]
        mu = jnp.mean(u, axis=-1, keepdims=True)
        uc = u - mu
        var = jnp.mean(uc * uc, axis=-1, keepdims=True)
        y = uc * lax.rsqrt(var + EPS) * lnw + lnb
        o_ref[pl.ds(r0, GRID_W), :] = _silu(y).astype(o_ref.dtype)
        return carry

    lax.fori_loop(0, n_seg, seg, 0)


def _conformer_call(p2, dw, lnw, lnb, *, tt):
    m = p2.shape[0]
    a_blk = (V_END + D_DELTA) // D_CONV
    return pl.pallas_call(
        functools.partial(_conformer_kernel, n_seg=tt // GRID_W),
        grid=(m // tt,),
        in_specs=[
            pl.BlockSpec((tt, D_CONV), lambda i: (i, a_blk)),
            pl.BlockSpec((tt, D_CONV), lambda i: (i, a_blk + 1)),
            pl.BlockSpec((CONF_W, D_CONV), lambda i: (0, 0)),
            pl.BlockSpec((1, D_CONV), lambda i: (0, 0)),
            pl.BlockSpec((1, D_CONV), lambda i: (0, 0)),
        ],
        out_specs=pl.BlockSpec((tt, D_CONV), lambda i: (i, 0)),
        out_shape=jax.ShapeDtypeStruct((m, D_CONV), BF16),
        scratch_shapes=[
            pltpu.VMEM((GRID_W + 2 * CONF_PAD, D_CONV), F32),
            pltpu.VMEM((GRID_W, D_CONV), F32),
        ],
        compiler_params=pltpu.CompilerParams(
            dimension_semantics=("arbitrary",), vmem_limit_bytes=VMEM_LIMIT),
        name="conformer",
    )(p2, p2, dw, lnw, lnb)


def _outproj_kernel(of_ref, ob_ref, z_ref, dnw_ref, mc_ref, w1_ref, w2_ref, x_ref, g_ref, nw_ref, o_ref, *,
                    tiles_per_batch):
    b = pl.program_id(0) // tiles_per_batch
    dnw = dnw_ref[...]
    heads = []
    for h in range(N_HEADS):
        lo = h * HEAD_DIM
        oh = of_ref[:, lo:lo + HEAD_DIM] + ob_ref[:, lo:lo + HEAD_DIM]
        heads.append((_rms(oh, dnw) * _silu(z_ref[:, lo:lo + HEAD_DIM])).astype(BF16))
    md = jnp.concatenate(heads, axis=1)
    y = (jnp.dot(md, w1_ref[...], preferred_element_type=F32)
         + jnp.dot(mc_ref[...], w2_ref[...], preferred_element_type=F32))
    g = g_ref[pl.ds(b, 1), :]
    o_ref[...] = x_ref[...] + g * _rms(y, nw_ref[...])


def _outproj_call(o_f, o_b, p2, dnw, mc, w_out, x2, mod, norm_w, *, tm, tiles_per_batch):
    m = x2.shape[0]
    z_blk = V_END // D_DELTA
    return pl.pallas_call(
        functools.partial(_outproj_kernel, tiles_per_batch=tiles_per_batch),
        grid=(m // tm,),
        in_specs=[
            pl.BlockSpec((tm, D_DELTA), lambda i: (i, 0)),
            pl.BlockSpec((tm, D_DELTA), lambda i: (i, 0)),
            pl.BlockSpec((tm, D_DELTA), lambda i: (i, z_blk)),
            pl.BlockSpec((1, HEAD_DIM), lambda i: (0, 0)),
            pl.BlockSpec((tm, D_CONV), lambda i: (i, 0)),
            pl.BlockSpec((D_DELTA, D_MODEL), lambda i: (0, 0)),
            pl.BlockSpec((D_CONV, D_MODEL), lambda i: (1, 0)),
            pl.BlockSpec((tm, D_MODEL), lambda i: (i, 0)),
            pl.BlockSpec((8, D_MODEL), lambda i: (0, 2)),
            pl.BlockSpec((1, D_MODEL), lambda i: (0, 0)),
        ],
        out_specs=pl.BlockSpec((tm, D_MODEL), lambda i: (i, 0)),
        out_shape=jax.ShapeDtypeStruct((m, D_MODEL), F32),
        compiler_params=pltpu.CompilerParams(
            dimension_semantics=("arbitrary",), vmem_limit_bytes=VMEM_LIMIT),
        name="out_proj",
    )(o_f, o_b, p2, dnw, mc, w_out, w_out, x2, mod, norm_w)


def _ffn_kernel(x_ref, nw_ref, sh_ref, sc_ref, g_ref, pw_ref, wg_ref, wu_ref, wd_ref, o_ref, h_ref, *,
                tiles_per_batch, n_k):
    i = pl.program_id(0)
    k = pl.program_id(1)
    b = i // tiles_per_batch

    @pl.when(k == 0)
    def _():
        sh = sh_ref[pl.ds(b, 1), :]
        sc = sc_ref[pl.ds(b, 1), :]
        h_ref[...] = (_rms(x_ref[...], nw_ref[...]) * (1.0 + sc) + sh).astype(BF16)

    hh = h_ref[...]
    gate = jnp.dot(hh, wg_ref[...], preferred_element_type=F32)
    up = jnp.dot(hh, wu_ref[...], preferred_element_type=F32)
    act = (_silu(gate) * up).astype(BF16)
    part = jnp.dot(act, wd_ref[...], preferred_element_type=F32)

    @pl.when(k == 0)
    def _():
        o_ref[...] = part

    @pl.when(k > 0)
    def _():
        o_ref[...] += part

    @pl.when(k == n_k - 1)
    def _():
        g = g_ref[pl.ds(b, 1), :]
        o_ref[...] = x_ref[...] + g * _rms(o_ref[...], pw_ref[...])


def _ffn_call(x2, mod, norm_pre, norm_post, wg, wu, wd, *, tm, tf, tiles_per_batch):
    m = x2.shape[0]
    dff = wg.shape[1]
    n_k = dff // tf
    return pl.pallas_call(
        functools.partial(_ffn_kernel, tiles_per_batch=tiles_per_batch, n_k=n_k),
        grid=(m // tm, n_k),
        in_specs=[
            pl.BlockSpec((tm, D_MODEL), lambda i, k: (i, 0)),
            pl.BlockSpec((1, D_MODEL), lambda i, k: (0, 0)),
            pl.BlockSpec((8, D_MODEL), lambda i, k: (0, 3)),
            pl.BlockSpec((8, D_MODEL), lambda i, k: (0, 4)),
            pl.BlockSpec((8, D_MODEL), lambda i, k: (0, 5)),
            pl.BlockSpec((1, D_MODEL), lambda i, k: (0, 0)),
            pl.BlockSpec((D_MODEL, tf), lambda i, k: (0, k)),
            pl.BlockSpec((D_MODEL, tf), lambda i, k: (0, k)),
            pl.BlockSpec((tf, D_MODEL), lambda i, k: (k, 0)),
        ],
        out_specs=pl.BlockSpec((tm, D_MODEL), lambda i, k: (i, 0)),
        out_shape=jax.ShapeDtypeStruct((m, D_MODEL), F32),
        scratch_shapes=[pltpu.VMEM((tm, D_MODEL), BF16)],
        compiler_params=pltpu.CompilerParams(
            dimension_semantics=("arbitrary", "arbitrary"), vmem_limit_bytes=VMEM_LIMIT),
        name="ffn",
    )(x2, norm_pre, mod, mod, mod, norm_post, wg, wu, wd)


@jax.jit
def _forward(x, c, ctx, c_ctx, w_mod, b_mod, mix_norm_pre, mix_norm_post, w_in, qkv_conv,
             a_log, dt_bias, delta_out_norm, conf_dw, conf_ln_w, conf_ln_b, w_out,
             ffn_norm_pre, ffn_norm_post, w_gate, w_up, w_down):
    bsz, seq, d = x.shape
    ctx_len = ctx.shape[1]
    assert d == D_MODEL and w_mod.shape[0] == 1, "single-layer kernel"
    assert seq % 1024 == 0 and ctx_len % CHUNK == 0 and bsz + 1 <= 8

    cc = jnp.zeros((8, D_MODEL), F32).at[:bsz].set(c).at[bsz].set(c_ctx)
    mod = _mod_call(cc, w_mod[0], b_mod[0][None, :])

    w_in0 = w_in[0]
    w_main = jnp.concatenate([w_in0[:, :V_END], w_in0[:, STATE_END:]], axis=1).astype(BF16)
    w_g = jnp.zeros((D_MODEL, LANES), BF16).at[:, :N_FEAT].set(w_in0[:, V_END:STATE_END].astype(BF16))
    prm = (jnp.zeros((8, LANES), F32)
           .at[0, 2 * N_HEADS:N_FEAT].set(a_log[0].reshape(-1))
           .at[1, 2 * N_HEADS:N_FEAT].set(dt_bias[0].reshape(-1)))

    x2 = x.reshape(bsz * seq, D_MODEL)
    ctx2 = ctx.reshape(bsz * ctx_len, D_MODEL)
    tm = 1024
    p, ba = _inproj_call(x2, mix_norm_pre, mod, w_main, w_g, n_cols=D_MAIN, tm=tm, tn=768,
                         tiles_per_mod_row=seq // tm, mod_row0=0)
    pc, bac = _inproj_call(ctx2, mix_norm_pre, mod, w_main, w_g, n_cols=V_END, tm=bsz * ctx_len, tn=768,
                           tiles_per_mod_row=1, mod_row0=bsz)

    fc, fr = _gatefeat_call(ba.reshape(bsz, seq, LANES), prm)
    fc_c, fr_c = _gatefeat_call(bac.reshape(bsz, ctx_len, LANES), prm)
    qkv = _qkvconv_call(p.reshape(bsz, seq, D_MAIN), qkv_conv[0], tile=256)
    qkv_c = _qkvconv_call(pc.reshape(bsz, ctx_len, V_END), qkv_conv[0], tile=256)

    zero_state = jnp.zeros((bsz, 2, N_HEADS, HEAD_DIM, HEAD_DIM), F32)
    _, _, s_ctx = _delta_call(qkv_c, fc_c, fr_c, zero_state, lb=ctx_len)
    o_f, o_b, _ = _delta_call(qkv, fc, fr, s_ctx, lb=512)

    mix_c = _conformer_call(p, conf_dw[0], conf_ln_w, conf_ln_b, tt=512)

    x1 = _outproj_call(o_f.reshape(bsz * seq, D_DELTA), o_b.reshape(bsz * seq, D_DELTA), p, delta_out_norm,
                       mix_c, w_out[0].astype(BF16), x2, mod, mix_norm_post,
                       tm=512, tiles_per_batch=seq // 512)
    out = _ffn_call(x1, mod, ffn_norm_pre, ffn_norm_post, w_gate[0].astype(BF16), w_up[0].astype(BF16),
                    w_down[0].astype(BF16), tm=512, tf=512, tiles_per_batch=seq // 512)
    return out.reshape(bsz, seq, D_MODEL)


def kernel(x, c, ctx, c_ctx, w_mod, b_mod, mix_norm_pre, mix_norm_post, w_in, qkv_conv, a_log, dt_bias,
           delta_out_norm, conf_dw, conf_ln_w, conf_ln_b, w_out, ffn_norm_pre, ffn_norm_post,
           w_gate, w_up, w_down):
    return _forward(x, c, ctx, c_ctx, w_mod, b_mod, mix_norm_pre, mix_norm_post, w_in, qkv_conv,
                    a_log, dt_bias, delta_out_norm, conf_dw, conf_ln_w, conf_ln_b, w_out,
                    ffn_norm_pre, ffn_norm_post, w_gate, w_up, w_down)
```

```python
import functools

import jax
import jax.numpy as jnp
from jax import lax
from jax.experimental import pallas as pl
from jax.experimental.pallas import tpu as pltpu

F32 = jnp.float32
BF16 = jnp.bfloat16
HIGHEST = lax.Precision.HIGHEST

D_MODEL = 2048
N_HEADS = 8
HEAD_DIM = 128
D_DELTA = N_HEADS * HEAD_DIM
D_CONV = D_MODEL - D_DELTA
GRID_W = 64
CHUNK = 64
SHORT_W = 5
CONF_W = 31
EPS = 1e-6
V_END = 3 * D_DELTA
STATE_END = V_END + 4 * N_HEADS
Z_END = STATE_END + D_DELTA
D_MAIN = V_END + D_DELTA + 2 * D_CONV
N_FEAT = 4 * N_HEADS
LANES = 128
VMEM_LIMIT = 56 * 1024 * 1024

NT_DIMS = (((1,), (1,)), ((), ()))
TN_DIMS = (((0,), (0,)), ((), ()))


def _mm(a, b, dims=None):
    a = a.astype(BF16)
    b = b.astype(BF16)
    if dims is None:
        return jnp.dot(a, b, preferred_element_type=F32)
    return lax.dot_general(a, b, dims, preferred_element_type=F32)


def _silu(x):
    return x * jax.nn.sigmoid(x)


def _rms(x, w):
    return x * lax.rsqrt(jnp.mean(x * x, axis=-1, keepdims=True) + EPS) * w


def _mod_kernel(c_ref, w_ref, b_ref, o_ref):
    s = _silu(c_ref[...])
    o_ref[...] = jnp.dot(s, w_ref[...], preferred_element_type=F32, precision=HIGHEST) + b_ref[...]


def _mod_call(cc, w_mod, b_mod):
    n = w_mod.shape[1]
    tn = 1024
    return pl.pallas_call(
        _mod_kernel,
        grid=(n // tn,),
        in_specs=[
            pl.BlockSpec((8, D_MODEL), lambda j: (0, 0)),
            pl.BlockSpec((D_MODEL, tn), lambda j: (0, j)),
            pl.BlockSpec((1, tn), lambda j: (0, j)),
        ],
        out_specs=pl.BlockSpec((8, tn), lambda j: (0, j)),
        out_shape=jax.ShapeDtypeStruct((8, n), F32),
        compiler_params=pltpu.CompilerParams(
            dimension_semantics=("arbitrary",), vmem_limit_bytes=VMEM_LIMIT),
        name="mod",
    )(cc, w_mod, b_mod)


def _inproj_kernel(x_ref, nw_ref, sh_ref, sc_ref, w_ref, wg_ref, o_ref, og_ref, h_ref, *,
                   tiles_per_mod_row, mod_row0):
    i = pl.program_id(0)
    j = pl.program_id(1)

    @pl.when(j == 0)
    def _():
        r = mod_row0 + i // tiles_per_mod_row
        sh = sh_ref[pl.ds(r, 1), :]
        sc = sc_ref[pl.ds(r, 1), :]
        hh = (_rms(x_ref[...], nw_ref[...]) * (1.0 + sc) + sh).astype(BF16)
        h_ref[...] = hh
        og_ref[...] = jnp.dot(hh, wg_ref[...], preferred_element_type=F32)

    o_ref[...] = jnp.dot(h_ref[...], w_ref[...], preferred_element_type=F32)


def _inproj_call(x2, norm_w, mod, w_main, w_g, *, n_cols, tm, tn, tiles_per_mod_row, mod_row0):
    m = x2.shape[0]
    kern = functools.partial(_inproj_kernel, tiles_per_mod_row=tiles_per_mod_row, mod_row0=mod_row0)
    return pl.pallas_call(
        kern,
        grid=(m // tm, n_cols // tn),
        in_specs=[
            pl.BlockSpec((tm, D_MODEL), lambda i, j: (i, 0)),
            pl.BlockSpec((1, D_MODEL), lambda i, j: (0, 0)),
            pl.BlockSpec((8, D_MODEL), lambda i, j: (0, 0)),
            pl.BlockSpec((8, D_MODEL), lambda i, j: (0, 1)),
            pl.BlockSpec((D_MODEL, tn), lambda i, j: (0, j)),
            pl.BlockSpec((D_MODEL, LANES), lambda i, j: (0, 0)),
        ],
        out_specs=[
            pl.BlockSpec((tm, tn), lambda i, j: (i, j)),
            pl.BlockSpec((tm, LANES), lambda i, j: (i, 0)),
        ],
        out_shape=[
            jax.ShapeDtypeStruct((m, n_cols), F32),
            jax.ShapeDtypeStruct((m, LANES), F32),
        ],
        scratch_shapes=[pltpu.VMEM((tm, D_MODEL), BF16)],
        compiler_params=pltpu.CompilerParams(
            dimension_semantics=("arbitrary", "arbitrary"), vmem_limit_bytes=VMEM_LIMIT),
        name="in_proj",
    )(x2, norm_w, mod, mod, w_main, w_g)


GATE_UNROLL = 4


def _gatefeat_kernel(ba_ref, prm_ref, fc_ref, fr_ref, *, n_chunks):
    lane = lax.broadcasted_iota(jnp.int32, (CHUNK, LANES), 1)
    rr = lax.broadcasted_iota(jnp.int32, (CHUNK, CHUNK), 0)
    cc = lax.broadcasted_iota(jnp.int32, (CHUNK, CHUNK), 1)
    ltri = (rr >= cc).astype(F32)
    utri = (rr <= cc).astype(F32)
    r2 = lax.broadcasted_iota(jnp.int32, (LANES, LANES), 0)
    c2 = lax.broadcasted_iota(jnp.int32, (LANES, LANES), 1)
    eye = (r2 == c2).astype(F32)
    neg_a = -jnp.exp(prm_ref[0:1, :])
    dtb = prm_ref[1:2, :]

    def body(it, carry):
        idx = [it * GATE_UNROLL + u for u in range(GATE_UNROLL)]
        starts = [pl.multiple_of(ci * CHUNK, CHUNK) for ci in idx]
        xs = [ba_ref[pl.ds(s0, CHUNK), :] for s0 in starts]
        gs = []
        for x in xs:
            y = x + dtb
            gs.append(neg_a * (jnp.maximum(y, 0.0) + jnp.log1p(jnp.exp(-jnp.abs(y)))))
        pres = [jnp.dot(ltri, g, preferred_element_type=F32, precision=HIGHEST) for g in gs]
        sufs = [jnp.dot(utri, g, preferred_element_type=F32, precision=HIGHEST) for g in gs]
        fs = [jnp.where(lane < 2 * N_HEADS, jax.nn.sigmoid(x), jnp.where(lane < 3 * N_HEADS, pre, suf))
              for x, pre, suf in zip(xs, pres, sufs)]
        fts = [lax.dot_general(eye, f, NT_DIMS, preferred_element_type=F32, precision=HIGHEST) for f in fs]
        for ci, s0, f, ft in zip(idx, starts, fs, fts):
            fc_ref[pl.ds(s0, CHUNK), :] = f
            fr_ref[ci] = ft[0:N_FEAT, :]
        return carry

    lax.fori_loop(0, n_chunks // GATE_UNROLL, body, 0)


def _gatefeat_call(ba, prm):
    b, l, _ = ba.shape
    n_chunks = l // CHUNK
    return pl.pallas_call(
        functools.partial(_gatefeat_kernel, n_chunks=n_chunks),
        grid=(b,),
        in_specs=[
            pl.BlockSpec((None, l, LANES), lambda i: (i, 0, 0)),
            pl.BlockSpec((8, LANES), lambda i: (0, 0)),
        ],
        out_specs=[
            pl.BlockSpec((None, l, LANES), lambda i: (i, 0, 0)),
            pl.BlockSpec((None, n_chunks, N_FEAT, CHUNK), lambda i: (i, 0, 0, 0)),
        ],
        out_shape=[
            jax.ShapeDtypeStruct((b, l, LANES), F32),
            jax.ShapeDtypeStruct((b, n_chunks, N_FEAT, CHUNK), F32),
        ],
        compiler_params=pltpu.CompilerParams(
            dimension_semantics=("arbitrary",), vmem_limit_bytes=VMEM_LIMIT),
        name="gate_feat",
    )(ba, prm)


QKV_PAD = 8


def _qkvconv_kernel(x_ref, w_ref, o_ref, pad_ref, *, seq, tile):
    j = pl.program_id(1)
    zeros = jnp.zeros((QKV_PAD, LANES), F32)
    pad_ref[0:QKV_PAD, :] = zeros
    pad_ref[QKV_PAD + seq:2 * QKV_PAD + seq, :] = zeros
    pad_ref[QKV_PAD:QKV_PAD + seq, :] = x_ref[...]
    w = w_ref[...]
    is_qk = j < 2 * N_HEADS
    for t0 in range(0, seq, tile):
        acc = None
        for tap in range(SHORT_W):
            start = t0 + QKV_PAD - SHORT_W // 2 + tap
            term = pad_ref[start:start + tile, :] * w[tap:tap + 1, :]
            acc = term if acc is None else acc + term
        y = _silu(acc)
        inv = lax.rsqrt(jnp.sum(y * y, axis=-1, keepdims=True) + EPS)
        o_ref[t0:t0 + tile, :] = y * jnp.where(is_qk, inv, 1.0)


def _qkvconv_call(p, conv_w, *, tile):
    b, l, _ = p.shape
    n_blocks = V_END // LANES
    return pl.pallas_call(
        functools.partial(_qkvconv_kernel, seq=l, tile=tile),
        grid=(b, n_blocks),
        in_specs=[
            pl.BlockSpec((None, l, LANES), lambda i, j: (i, 0, j)),
            pl.BlockSpec((SHORT_W, LANES), lambda i, j: (0, j)),
        ],
        out_specs=pl.BlockSpec((None, l, LANES), lambda i, j: (i, 0, j)),
        out_shape=jax.ShapeDtypeStruct((b, l, V_END), F32),
        scratch_shapes=[pltpu.VMEM((l + 2 * QKV_PAD, LANES), F32)],
        compiler_params=pltpu.CompilerParams(
            dimension_semantics=("arbitrary", "arbitrary"), vmem_limit_bytes=VMEM_LIMIT),
        name="qkv_conv",
    )(p, conv_w)


def _delta_kernel(xf_ref, xb_ref, fcf_ref, fcb_ref, frf_ref, frb_ref, init_ref,
                  of_ref, ob_ref, st_ref, *, n_pos):
    @pl.when(pl.program_id(1) == 0)
    def _():
        st_ref[...] = init_ref[...]

    rr = lax.broadcasted_iota(jnp.int32, (CHUNK, CHUNK), 0)
    cc = lax.broadcasted_iota(jnp.int32, (CHUNK, CHUNK), 1)
    eye = (rr == cc).astype(F32)
    q_scale = HEAD_DIM ** -0.5

    class Chain:
        def __init__(self, x_ref, fc_ref, fr_ref, o_ref, c, h, forward):
            self.x_ref, self.fc_ref, self.fr_ref, self.o_ref = x_ref, fc_ref, fr_ref, o_ref
            self.c, self.h, self.forward = c, h, forward
            self.d = 0 if forward else 1
            self.s0 = pl.multiple_of(c * CHUNK, CHUNK)
            self.lo = h * HEAD_DIM

        def tile(self, which):
            off = which * D_DELTA + self.lo
            return self.x_ref[pl.ds(self.s0, CHUNK), off:off + HEAD_DIM]

        def feats(self):
            f = self.fc_ref[pl.ds(self.s0, CHUNK), :]
            ib = self.d * N_HEADS + self.h
            ig = (2 + self.d) * N_HEADS + self.h
            beta = f[:, ib:ib + 1]
            gc = f[:, ig:ig + 1]
            gr = self.fr_ref[self.c, ig:ig + 1, :]
            g_end = gr[:, CHUNK - 1:CHUNK] if self.forward else gr[:, 0:1]
            return beta, gc, gr, g_end

    def stage_gram(ch):
        beta, gc, gr, _ = ch.feats()
        k = ch.tile(1)
        ch.gram = _mm(jnp.concatenate([ch.tile(0) * q_scale, k * beta], axis=0), k, NT_DIMS)
        incl = (rr >= cc) if ch.forward else (rr <= cc)
        ch.decay = jnp.where(incl, jnp.exp(jnp.where(incl, gc - gr, 0.0)), 0.0)

    def stage_power0(ch):
        strict = (rr > cc) if ch.forward else (rr < cc)
        ch.attn = (ch.gram[:CHUNK] * ch.decay).astype(BF16)
        neg = -jnp.where(strict, ch.gram[CHUNK:] * ch.decay, 0.0)
        ch.neg = neg.astype(BF16)
        ch.t_inv = eye + neg
        ch.pw = _mm(ch.neg, ch.neg)
        del ch.gram, ch.decay

    def stage_power(ch):
        both = _mm(jnp.concatenate([ch.pw, ch.t_inv], axis=0), ch.pw)
        ch.t_inv = ch.t_inv + both[CHUNK:]
        ch.pw = both[:CHUNK]

    def stage_power_last(ch):
        ch.t_inv = (ch.t_inv + _mm(ch.t_inv, ch.pw)).astype(BF16)
        del ch.pw

    def stage_newton_residual(ch):
        ch.err = eye - ch.t_inv.astype(F32) + _mm(ch.neg, ch.t_inv)
        del ch.neg

    def stage_newton_apply(ch):
        ch.t_inv = (ch.t_inv.astype(F32) + _mm(ch.t_inv, ch.err)).astype(BF16)
        del ch.err

    def stage_predict(ch):
        beta, gc, _, _ = ch.feats()
        eg = jnp.exp(gc)
        kbg = ch.tile(1) * (beta * eg)
        qd = ch.tile(0) * (q_scale * eg)
        ps = _mm(jnp.concatenate([kbg, qd], axis=0), st_ref[ch.d, ch.h])
        ch.resid = ch.tile(2) * beta - ps[:CHUNK]
        ch.qs = ps[CHUNK:]

    def stage_solve(ch):
        ch.v_new = _mm(ch.t_inv, ch.resid).astype(BF16)
        del ch.resid, ch.t_inv

    def stage_update(ch):
        _, gc, _, g_end = ch.feats()
        ch.o_ref[pl.ds(ch.s0, CHUNK), ch.lo:ch.lo + HEAD_DIM] = ch.qs + _mm(ch.attn, ch.v_new)
        k_dec = ch.tile(1) * jnp.exp(g_end - gc)
        st_ref[ch.d, ch.h] = st_ref[ch.d, ch.h] * jnp.exp(g_end) + _mm(k_dec, ch.v_new, TN_DIMS)

    stages = ([stage_gram, stage_power0] + [stage_power] * 4
              + [stage_power_last, stage_newton_residual, stage_newton_apply,
                 stage_predict, stage_solve, stage_update])

    def body(c, carry):
        cb = n_pos - 1 - c
        chains = []
        for h in range(N_HEADS):
            chains.append(Chain(xf_ref, fcf_ref, frf_ref, of_ref, c, h, True))
            chains.append(Chain(xb_ref, fcb_ref, frb_ref, ob_ref, cb, h, False))
        for stage in stages:
            for ch in chains:
                stage(ch)
        return carry

    lax.fori_loop(0, n_pos, body, 0)


def _delta_call(qkv, fc, fr, init, *, lb):
    b, l, _ = qkv.shape
    nb = l // lb
    n_pos = lb // CHUNK
    fwd3 = lambda i, j: (i, j, 0)
    bwd3 = lambda i, j: (i, nb - 1 - j, 0)
    st_spec = pl.BlockSpec((None, 2, N_HEADS, HEAD_DIM, HEAD_DIM), lambda i, j: (i, 0, 0, 0, 0))
    return pl.pallas_call(
        functools.partial(_delta_kernel, n_pos=n_pos),
        grid=(b, nb),
        in_specs=[
            pl.BlockSpec((None, lb, V_END), fwd3),
            pl.BlockSpec((None, lb, V_END), bwd3),
            pl.BlockSpec((None, lb, LANES), fwd3),
            pl.BlockSpec((None, lb, LANES), bwd3),
            pl.BlockSpec((None, n_pos, N_FEAT, CHUNK), lambda i, j: (i, j, 0, 0)),
            pl.BlockSpec((None, n_pos, N_FEAT, CHUNK), lambda i, j: (i, nb - 1 - j, 0, 0)),
            st_spec,
        ],
        out_specs=[
            pl.BlockSpec((None, lb, D_DELTA), fwd3),
            pl.BlockSpec((None, lb, D_DELTA), bwd3),
            st_spec,
        ],
        out_shape=[
            jax.ShapeDtypeStruct((b, l, D_DELTA), F32),
            jax.ShapeDtypeStruct((b, l, D_DELTA), F32),
            jax.ShapeDtypeStruct((b, 2, N_HEADS, HEAD_DIM, HEAD_DIM), F32),
        ],
        compiler_params=pltpu.CompilerParams(
            dimension_semantics=("arbitrary", "arbitrary"), vmem_limit_bytes=VMEM_LIMIT),
        name="delta",
    )(qkv, qkv, fc, fc, fr, fr, init)


CONF_PAD = 16
CONF_CCHUNK = 256
SUBLANES = 8
CONF_SHIFT_ROWS = GRID_W + 2 * CONF_PAD - SUBLANES


def _conformer_kernel(a_ref, g_ref, dw_ref, lnw_ref, lnb_ref, o_ref, pad_ref, sh_ref, cv_ref, *, n_seg):
    zeros = jnp.zeros((CONF_PAD, D_CONV), F32)
    pad_ref[0:CONF_PAD, :] = zeros
    pad_ref[CONF_PAD + GRID_W:2 * CONF_PAD + GRID_W, :] = zeros
    lnw = lnw_ref[...]
    lnb = lnb_ref[...]

    def seg(s, carry):
        r0 = pl.multiple_of(s * GRID_W, GRID_W)
        pad_ref[CONF_PAD:CONF_PAD + GRID_W, :] = (
            a_ref[pl.ds(r0, GRID_W), :] * jax.nn.sigmoid(g_ref[pl.ds(r0, GRID_W), :]))
        for c0 in range(0, D_CONV, CONF_CCHUNK):
            cols = slice(c0, c0 + CONF_CCHUNK)
            for r in range(1, SUBLANES):
                sh_ref[r, :, cols] = pad_ref[r:r + CONF_SHIFT_ROWS, cols]
            acc = None
            for tap in range(CONF_W):
                start = CONF_PAD - CONF_W // 2 + tap
                r = start % SUBLANES
                base = start - r
                win = (pad_ref[base:base + GRID_W, cols] if r == 0
                       else sh_ref[r, base:base + GRID_W, cols])
                term = win * dw_ref[tap:tap + 1, cols]
                acc = term if acc is None else acc + term
            cv_ref[:, cols] = acc
        u = cv_ref[...]
        mu = jnp.mean(u, axis=-1, keepdims=True)
        uc = u - mu
        var = jnp.mean(uc * uc, axis=-1, keepdims=True)
        y = uc * lax.rsqrt(var + EPS) * lnw + lnb
        o_ref[pl.ds(r0, GRID_W), :] = _silu(y).astype(o_ref.dtype)
        return carry

    lax.fori_loop(0, n_seg, seg, 0)


def _conformer_call(p2, dw, lnw, lnb, *, tt):
    m = p2.shape[0]
    a_blk = (V_END + D_DELTA) // D_CONV
    return pl.pallas_call(
        functools.partial(_conformer_kernel, n_seg=tt // GRID_W),
        grid=(m // tt,),
        in_specs=[
            pl.BlockSpec((tt, D_CONV), lambda i: (i, a_blk)),
            pl.BlockSpec((tt, D_CONV), lambda i: (i, a_blk + 1)),
            pl.BlockSpec((CONF_W, D_CONV), lambda i: (0, 0)),
            pl.BlockSpec((1, D_CONV), lambda i: (0, 0)),
            pl.BlockSpec((1, D_CONV), lambda i: (0, 0)),
        ],
        out_specs=pl.BlockSpec((tt, D_CONV), lambda i: (i, 0)),
        out_shape=jax.ShapeDtypeStruct((m, D_CONV), BF16),
        scratch_shapes=[
            pltpu.VMEM((GRID_W + 2 * CONF_PAD, D_CONV), F32),
            pltpu.VMEM((SUBLANES, CONF_SHIFT_ROWS, D_CONV), F32),
            pltpu.VMEM((GRID_W, D_CONV), F32),
        ],
        compiler_params=pltpu.CompilerParams(
            dimension_semantics=("arbitrary",), vmem_limit_bytes=VMEM_LIMIT),
        name="conformer",
    )(p2, p2, dw, lnw, lnb)


def _outproj_kernel(of_ref, ob_ref, z_ref, dnw_ref, mc_ref, w1_ref, w2_ref, x_ref, g_ref, nw_ref, o_ref, *,
                    tiles_per_batch):
    b = pl.program_id(0) // tiles_per_batch
    dnw = dnw_ref[...]
    heads = []
    for h in range(N_HEADS):
        lo = h * HEAD_DIM
        oh = of_ref[:, lo:lo + HEAD_DIM] + ob_ref[:, lo:lo + HEAD_DIM]
        heads.append((_rms(oh, dnw) * _silu(z_ref[:, lo:lo + HEAD_DIM])).astype(BF16))
    md = jnp.concatenate(heads, axis=1)
    y = (jnp.dot(md, w1_ref[...], preferred_element_type=F32)
         + jnp.dot(mc_ref[...], w2_ref[...], preferred_element_type=F32))
    g = g_ref[pl.ds(b, 1), :]
    o_ref[...] = x_ref[...] + g * _rms(y, nw_ref[...])


def _outproj_call(o_f, o_b, p2, dnw, mc, w_out, x2, mod, norm_w, *, tm, tiles_per_batch):
    m = x2.shape[0]
    z_blk = V_END // D_DELTA
    return pl.pallas_call(
        functools.partial(_outproj_kernel, tiles_per_batch=tiles_per_batch),
        grid=(m // tm,),
        in_specs=[
            pl.BlockSpec((tm, D_DELTA), lambda i: (i, 0)),
            pl.BlockSpec((tm, D_DELTA), lambda i: (i, 0)),
            pl.BlockSpec((tm, D_DELTA), lambda i: (i, z_blk)),
            pl.BlockSpec((1, HEAD_DIM), lambda i: (0, 0)),
            pl.BlockSpec((tm, D_CONV), lambda i: (i, 0)),
            pl.BlockSpec((D_DELTA, D_MODEL), lambda i: (0, 0)),
            pl.BlockSpec((D_CONV, D_MODEL), lambda i: (1, 0)),
            pl.BlockSpec((tm, D_MODEL), lambda i: (i, 0)),
            pl.BlockSpec((8, D_MODEL), lambda i: (0, 2)),
            pl.BlockSpec((1, D_MODEL), lambda i: (0, 0)),
        ],
        out_specs=pl.BlockSpec((tm, D_MODEL), lambda i: (i, 0)),
        out_shape=jax.ShapeDtypeStruct((m, D_MODEL), F32),
        compiler_params=pltpu.CompilerParams(
            dimension_semantics=("arbitrary",), vmem_limit_bytes=VMEM_LIMIT),
        name="out_proj",
    )(o_f, o_b, p2, dnw, mc, w_out, w_out, x2, mod, norm_w)


def _ffn_kernel(x_ref, nw_ref, sh_ref, sc_ref, g_ref, pw_ref, wg_ref, wu_ref, wd_ref, o_ref, h_ref, *,
                tiles_per_batch, n_k):
    i = pl.program_id(0)
    k = pl.program_id(1)
    b = i // tiles_per_batch

    @pl.when(k == 0)
    def _():
        sh = sh_ref[pl.ds(b, 1), :]
        sc = sc_ref[pl.ds(b, 1), :]
        h_ref[...] = (_rms(x_ref[...], nw_ref[...]) * (1.0 + sc) + sh).astype(BF16)
        o_ref[...] = jnp.zeros(o_ref.shape, F32)

    hh = h_ref[...]
    gate = jnp.dot(hh, wg_ref[...], preferred_element_type=F32)
    up = jnp.dot(hh, wu_ref[...], preferred_element_type=F32)
    act = (_silu(gate) * up).astype(BF16)
    o_ref[...] += jnp.dot(act, wd_ref[...], preferred_element_type=F32)

    @pl.when(k == n_k - 1)
    def _():
        g = g_ref[pl.ds(b, 1), :]
        o_ref[...] = x_ref[...] + g * _rms(o_ref[...], pw_ref[...])


def _ffn_call(x2, mod, norm_pre, norm_post, wg, wu, wd, *, tm, tf, tiles_per_batch):
    m = x2.shape[0]
    dff = wg.shape[1]
    n_k = dff // tf
    return pl.pallas_call(
        functools.partial(_ffn_kernel, tiles_per_batch=tiles_per_batch, n_k=n_k),
        grid=(m // tm, n_k),
        in_specs=[
            pl.BlockSpec((tm, D_MODEL), lambda i, k: (i, 0)),
            pl.BlockSpec((1, D_MODEL), lambda i, k: (0, 0)),
            pl.BlockSpec((8, D_MODEL), lambda i, k: (0, 3)),
            pl.BlockSpec((8, D_MODEL), lambda i, k: (0, 4)),
            pl.BlockSpec((8, D_MODEL), lambda i, k: (0, 5)),
            pl.BlockSpec((1, D_MODEL), lambda i, k: (0, 0)),
            pl.BlockSpec((D_MODEL, tf), lambda i, k: (0, k)),
            pl.BlockSpec((D_MODEL, tf), lambda i, k: (0, k)),
            pl.BlockSpec((tf, D_MODEL), lambda i, k: (k, 0)),
        ],
        out_specs=pl.BlockSpec((tm, D_MODEL), lambda i, k: (i, 0)),
        out_shape=jax.ShapeDtypeStruct((m, D_MODEL), F32),
        scratch_shapes=[pltpu.VMEM((tm, D_MODEL), BF16)],
        compiler_params=pltpu.CompilerParams(
            dimension_semantics=("arbitrary", "arbitrary"), vmem_limit_bytes=VMEM_LIMIT),
        name="ffn",
    )(x2, norm_pre, mod, mod, mod, norm_post, wg, wu, wd)


@jax.jit
def _forward(x, c, ctx, c_ctx, w_mod, b_mod, mix_norm_pre, mix_norm_post, w_in, qkv_conv,
             a_log, dt_bias, delta_out_norm, conf_dw, conf_ln_w, conf_ln_b, w_out,
             ffn_norm_pre, ffn_norm_post, w_gate, w_up, w_down):
    bsz, seq, d = x.shape
    ctx_len = ctx.shape[1]
    assert d == D_MODEL and w_mod.shape[0] == 1, "single-layer kernel"
    assert seq % 1024 == 0 and ctx_len % (CHUNK * GATE_UNROLL) == 0 and bsz + 1 <= 8

    cc = jnp.zeros((8, D_MODEL), F32).at[:bsz].set(c).at[bsz].set(c_ctx)
    mod = _mod_call(cc, w_mod[0], b_mod[0][None, :])

    w_in0 = w_in[0]
    w_main = jnp.concatenate([w_in0[:, :V_END], w_in0[:, STATE_END:]], axis=1).astype(BF16)
    w_g = jnp.zeros((D_MODEL, LANES), BF16).at[:, :N_FEAT].set(w_in0[:, V_END:STATE_END].astype(BF16))
    prm = (jnp.zeros((8, LANES), F32)
           .at[0, 2 * N_HEADS:N_FEAT].set(a_log[0].reshape(-1))
           .at[1, 2 * N_HEADS:N_FEAT].set(dt_bias[0].reshape(-1)))

    x2 = x.reshape(bsz * seq, D_MODEL)
    ctx2 = ctx.reshape(bsz * ctx_len, D_MODEL)
    tm = 1024
    p, ba = _inproj_call(x2, mix_norm_pre, mod, w_main, w_g, n_cols=D_MAIN, tm=tm, tn=1536,
                         tiles_per_mod_row=seq // tm, mod_row0=0)
    pc, bac = _inproj_call(ctx2, mix_norm_pre, mod, w_main, w_g, n_cols=V_END, tm=bsz * ctx_len, tn=1536,
                           tiles_per_mod_row=1, mod_row0=bsz)

    fc, fr = _gatefeat_call(ba.reshape(bsz, seq, LANES), prm)
    fc_c, fr_c = _gatefeat_call(bac.reshape(bsz, ctx_len, LANES), prm)
    qkv = _qkvconv_call(p.reshape(bsz, seq, D_MAIN), qkv_conv[0], tile=256)
    qkv_c = _qkvconv_call(pc.reshape(bsz, ctx_len, V_END), qkv_conv[0], tile=256)

    zero_state = jnp.zeros((bsz, 2, N_HEADS, HEAD_DIM, HEAD_DIM), F32)
    _, _, s_ctx = _delta_call(qkv_c, fc_c, fr_c, zero_state, lb=ctx_len)
    o_f, o_b, _ = _delta_call(qkv, fc, fr, s_ctx, lb=512)

    mix_c = _conformer_call(p, conf_dw[0], conf_ln_w, conf_ln_b, tt=512)

    x1 = _outproj_call(o_f.reshape(bsz * seq, D_DELTA), o_b.reshape(bsz * seq, D_DELTA), p, delta_out_norm,
                       mix_c, w_out[0].astype(BF16), x2, mod, mix_norm_post,
                       tm=512, tiles_per_batch=seq // 512)
    out = _ffn_call(x1, mod, ffn_norm_pre, ffn_norm_post, w_gate[0].astype(BF16), w_up[0].astype(BF16),
                    w_down[0].astype(BF16), tm=512, tf=512, tiles_per_batch=seq // 512)
    return out.reshape(bsz, seq, D_MODEL)


def kernel(x, c, ctx, c_ctx, w_mod, b_mod, mix_norm_pre, mix_norm_post, w_in, qkv_conv, a_log, dt_bias,
           delta_out_norm, conf_dw, conf_ln_w, conf_ln_b, w_out, ffn_norm_pre, ffn_norm_post,
           w_gate, w_up, w_down):
    return _forward(x, c, ctx, c_ctx, w_mod, b_mod, mix_norm_pre, mix_norm_post, w_in, qkv_conv,
                    a_log, dt_bias, delta_out_norm, conf_dw, conf_ln_w, conf_ln_b, w_out,
                    ffn_norm_pre, ffn_norm_post, w_gate, w_up, w_down)
```

```python
import functools

import jax
import jax.numpy as jnp
from jax import lax
from jax.experimental import pallas as pl
from jax.experimental.pallas import tpu as pltpu

F32 = jnp.float32
BF16 = jnp.bfloat16
HIGHEST = lax.Precision.HIGHEST

D_MODEL = 2048
N_HEADS = 8
HEAD_DIM = 128
D_DELTA = N_HEADS * HEAD_DIM
D_CONV = D_MODEL - D_DELTA
GRID_W = 64
CHUNK = 64
SHORT_W = 5
CONF_W = 31
EPS = 1e-6
V_END = 3 * D_DELTA
STATE_END = V_END + 4 * N_HEADS
Z_END = STATE_END + D_DELTA
D_MAIN = V_END + D_DELTA + 2 * D_CONV
N_FEAT = 4 * N_HEADS
LANES = 128
VMEM_LIMIT = 56 * 1024 * 1024

NT_DIMS = (((1,), (1,)), ((), ()))
TN_DIMS = (((0,), (0,)), ((), ()))


def _mm(a, b, dims=None):
    a = a.astype(BF16)
    b = b.astype(BF16)
    if dims is None:
        return jnp.dot(a, b, preferred_element_type=F32)
    return lax.dot_general(a, b, dims, preferred_element_type=F32)


def _silu(x):
    return x * jax.nn.sigmoid(x)


def _rms(x, w):
    return x * lax.rsqrt(jnp.mean(x * x, axis=-1, keepdims=True) + EPS) * w


def _mod_kernel(c_ref, w_ref, b_ref, o_ref):
    s = _silu(c_ref[...])
    o_ref[...] = jnp.dot(s, w_ref[...], preferred_element_type=F32, precision=HIGHEST) + b_ref[...]


def _mod_call(cc, w_mod, b_mod):
    n = w_mod.shape[1]
    tn = 1024
    return pl.pallas_call(
        _mod_kernel,
        grid=(n // tn,),
        in_specs=[
            pl.BlockSpec((8, D_MODEL), lambda j: (0, 0)),
            pl.BlockSpec((D_MODEL, tn), lambda j: (0, j)),
            pl.BlockSpec((1, tn), lambda j: (0, j)),
        ],
        out_specs=pl.BlockSpec((8, tn), lambda j: (0, j)),
        out_shape=jax.ShapeDtypeStruct((8, n), F32),
        compiler_params=pltpu.CompilerParams(
            dimension_semantics=("arbitrary",), vmem_limit_bytes=VMEM_LIMIT),
        name="mod",
    )(cc, w_mod, b_mod)


def _inproj_kernel(x_ref, nw_ref, sh_ref, sc_ref, w_ref, wg_ref, o_ref, og_ref, h_ref, *,
                   tiles_per_mod_row, mod_row0):
    i = pl.program_id(0)
    j = pl.program_id(1)

    @pl.when(j == 0)
    def _():
        r = mod_row0 + i // tiles_per_mod_row
        sh = sh_ref[pl.ds(r, 1), :]
        sc = sc_ref[pl.ds(r, 1), :]
        hh = (_rms(x_ref[...], nw_ref[...]) * (1.0 + sc) + sh).astype(BF16)
        h_ref[...] = hh
        og_ref[...] = jnp.dot(hh, wg_ref[...], preferred_element_type=F32)

    o_ref[...] = jnp.dot(h_ref[...], w_ref[...], preferred_element_type=F32)


def _inproj_call(x2, norm_w, mod, w_main, w_g, *, n_cols, tm, tn, tiles_per_mod_row, mod_row0):
    m = x2.shape[0]
    kern = functools.partial(_inproj_kernel, tiles_per_mod_row=tiles_per_mod_row, mod_row0=mod_row0)
    return pl.pallas_call(
        kern,
        grid=(m // tm, n_cols // tn),
        in_specs=[
            pl.BlockSpec((tm, D_MODEL), lambda i, j: (i, 0)),
            pl.BlockSpec((1, D_MODEL), lambda i, j: (0, 0)),
            pl.BlockSpec((8, D_MODEL), lambda i, j: (0, 0)),
            pl.BlockSpec((8, D_MODEL), lambda i, j: (0, 1)),
            pl.BlockSpec((D_MODEL, tn), lambda i, j: (0, j)),
            pl.BlockSpec((D_MODEL, LANES), lambda i, j: (0, 0)),
        ],
        out_specs=[
            pl.BlockSpec((tm, tn), lambda i, j: (i, j)),
            pl.BlockSpec((tm, LANES), lambda i, j: (i, 0)),
        ],
        out_shape=[
            jax.ShapeDtypeStruct((m, n_cols), F32),
            jax.ShapeDtypeStruct((m, LANES), F32),
        ],
        scratch_shapes=[pltpu.VMEM((tm, D_MODEL), BF16)],
        compiler_params=pltpu.CompilerParams(
            dimension_semantics=("arbitrary", "arbitrary"), vmem_limit_bytes=VMEM_LIMIT),
        name="in_proj",
    )(x2, norm_w, mod, mod, w_main, w_g)


GATE_UNROLL = 4


def _gatefeat_kernel(ba_ref, prm_ref, fc_ref, fr_ref, *, n_chunks):
    lane = lax.broadcasted_iota(jnp.int32, (CHUNK, LANES), 1)
    rr = lax.broadcasted_iota(jnp.int32, (CHUNK, CHUNK), 0)
    cc = lax.broadcasted_iota(jnp.int32, (CHUNK, CHUNK), 1)
    ltri = (rr >= cc).astype(F32)
    utri = (rr <= cc).astype(F32)
    r2 = lax.broadcasted_iota(jnp.int32, (LANES, LANES), 0)
    c2 = lax.broadcasted_iota(jnp.int32, (LANES, LANES), 1)
    eye = (r2 == c2).astype(F32)
    neg_a = -jnp.exp(prm_ref[0:1, :])
    dtb = prm_ref[1:2, :]

    def body(it, carry):
        idx = [it * GATE_UNROLL + u for u in range(GATE_UNROLL)]
        starts = [pl.multiple_of(ci * CHUNK, CHUNK) for ci in idx]
        xs = [ba_ref[pl.ds(s0, CHUNK), :] for s0 in starts]
        gs = []
        for x in xs:
            y = x + dtb
            gs.append(neg_a * (jnp.maximum(y, 0.0) + jnp.log1p(jnp.exp(-jnp.abs(y)))))
        pres = [jnp.dot(ltri, g, preferred_element_type=F32, precision=HIGHEST) for g in gs]
        sufs = [jnp.dot(utri, g, preferred_element_type=F32, precision=HIGHEST) for g in gs]
        fs = [jnp.where(lane < 2 * N_HEADS, jax.nn.sigmoid(x), jnp.where(lane < 3 * N_HEADS, pre, suf))
              for x, pre, suf in zip(xs, pres, sufs)]
        fts = [lax.dot_general(eye, f, NT_DIMS, preferred_element_type=F32, precision=HIGHEST) for f in fs]
        for ci, s0, f, ft in zip(idx, starts, fs, fts):
            fc_ref[pl.ds(s0, CHUNK), :] = f
            fr_ref[ci] = ft[0:N_FEAT, :]
        return carry

    lax.fori_loop(0, n_chunks // GATE_UNROLL, body, 0)


def _gatefeat_call(ba, prm):
    b, l, _ = ba.shape
    n_chunks = l // CHUNK
    return pl.pallas_call(
        functools.partial(_gatefeat_kernel, n_chunks=n_chunks),
        grid=(b,),
        in_specs=[
            pl.BlockSpec((None, l, LANES), lambda i: (i, 0, 0)),
            pl.BlockSpec((8, LANES), lambda i: (0, 0)),
        ],
        out_specs=[
            pl.BlockSpec((None, l, LANES), lambda i: (i, 0, 0)),
            pl.BlockSpec((None, n_chunks, N_FEAT, CHUNK), lambda i: (i, 0, 0, 0)),
        ],
        out_shape=[
            jax.ShapeDtypeStruct((b, l, LANES), F32),
            jax.ShapeDtypeStruct((b, n_chunks, N_FEAT, CHUNK), F32),
        ],
        compiler_params=pltpu.CompilerParams(
            dimension_semantics=("arbitrary",), vmem_limit_bytes=VMEM_LIMIT),
        name="gate_feat",
    )(ba, prm)


QKV_PAD = 8


def _qkvconv_kernel(x_ref, w_ref, o_ref, pad_ref, *, seq, tile):
    j = pl.program_id(1)
    zeros = jnp.zeros((QKV_PAD, LANES), F32)
    pad_ref[0:QKV_PAD, :] = zeros
    pad_ref[QKV_PAD + seq:2 * QKV_PAD + seq, :] = zeros
    pad_ref[QKV_PAD:QKV_PAD + seq, :] = x_ref[...]
    w = w_ref[...]
    is_qk = j < 2 * N_HEADS
    for t0 in range(0, seq, tile):
        acc = None
        for tap in range(SHORT_W):
            start = t0 + QKV_PAD - SHORT_W // 2 + tap
            term = pad_ref[start:start + tile, :] * w[tap:tap + 1, :]
            acc = term if acc is None else acc + term
        y = _silu(acc)
        inv = lax.rsqrt(jnp.sum(y * y, axis=-1, keepdims=True) + EPS)
        o_ref[t0:t0 + tile, :] = (y * jnp.where(is_qk, inv, 1.0)).astype(o_ref.dtype)


def _qkvconv_call(p, conv_w, *, tile):
    b, l, _ = p.shape
    n_blocks = V_END // LANES
    return pl.pallas_call(
        functools.partial(_qkvconv_kernel, seq=l, tile=tile),
        grid=(b, n_blocks),
        in_specs=[
            pl.BlockSpec((None, l, LANES), lambda i, j: (i, 0, j)),
            pl.BlockSpec((SHORT_W, LANES), lambda i, j: (0, j)),
        ],
        out_specs=pl.BlockSpec((None, l, LANES), lambda i, j: (i, 0, j)),
        out_shape=jax.ShapeDtypeStruct((b, l, V_END), BF16),
        scratch_shapes=[pltpu.VMEM((l + 2 * QKV_PAD, LANES), F32)],
        compiler_params=pltpu.CompilerParams(
            dimension_semantics=("arbitrary", "arbitrary"), vmem_limit_bytes=VMEM_LIMIT),
        name="qkv_conv",
    )(p, conv_w)


def _delta_kernel(xf_ref, xb_ref, fcf_ref, fcb_ref, frf_ref, frb_ref, init_ref,
                  of_ref, ob_ref, st_ref, *, n_pos):
    @pl.when(pl.program_id(1) == 0)
    def _():
        st_ref[...] = init_ref[...]

    rr = lax.broadcasted_iota(jnp.int32, (CHUNK, CHUNK), 0)
    cc = lax.broadcasted_iota(jnp.int32, (CHUNK, CHUNK), 1)
    eye = (rr == cc).astype(F32)
    q_scale = HEAD_DIM ** -0.5

    class Chain:
        def __init__(self, x_ref, fc_ref, fr_ref, o_ref, c, h, forward):
            self.x_ref, self.fc_ref, self.fr_ref, self.o_ref = x_ref, fc_ref, fr_ref, o_ref
            self.c, self.h, self.forward = c, h, forward
            self.d = 0 if forward else 1
            self.s0 = pl.multiple_of(c * CHUNK, CHUNK)
            self.lo = h * HEAD_DIM

        def tile(self, which):
            off = which * D_DELTA + self.lo
            return self.x_ref[pl.ds(self.s0, CHUNK), off:off + HEAD_DIM].astype(F32)

        def cols(self):
            f = self.fc_ref[pl.ds(self.s0, CHUNK), :]
            ib = self.d * N_HEADS + self.h
            ig = (2 + self.d) * N_HEADS + self.h
            shape = (CHUNK, HEAD_DIM)
            return jnp.broadcast_to(f[:, ib:ib + 1], shape), jnp.broadcast_to(f[:, ig:ig + 1], shape)

        def g_row(self):
            ig = (2 + self.d) * N_HEADS + self.h
            return self.fr_ref[self.c, ig:ig + 1, :]

    def stage_gram(ch):
        ch.beta_b, ch.gc_b = ch.cols()
        k = ch.tile(1)
        ch.gram = _mm(jnp.concatenate([ch.tile(0) * q_scale, k * ch.beta_b], axis=0), k, NT_DIMS)
        incl = (rr >= cc) if ch.forward else (rr <= cc)
        ch.decay = jnp.where(incl, jnp.exp(jnp.where(incl, ch.gc_b[:, :CHUNK] - ch.g_row(), 0.0)), 0.0)

    def stage_power0(ch):
        strict = (rr > cc) if ch.forward else (rr < cc)
        ch.attn = (ch.gram[:CHUNK] * ch.decay).astype(BF16)
        neg = -jnp.where(strict, ch.gram[CHUNK:] * ch.decay, 0.0)
        ch.neg = neg.astype(BF16)
        ch.t_inv = eye + neg
        ch.pw = _mm(ch.neg, ch.neg)
        del ch.gram, ch.decay

    def stage_power(ch):
        both = _mm(jnp.concatenate([ch.pw, ch.t_inv], axis=0), ch.pw)
        ch.t_inv = ch.t_inv + both[CHUNK:]
        ch.pw = both[:CHUNK]

    def stage_power_last(ch):
        ch.t_inv = (ch.t_inv + _mm(ch.t_inv, ch.pw)).astype(BF16)
        del ch.pw

    def stage_newton_residual(ch):
        ch.err = eye - ch.t_inv.astype(F32) + _mm(ch.neg, ch.t_inv)
        del ch.neg

    def stage_newton_apply(ch):
        ch.t_inv = (ch.t_inv.astype(F32) + _mm(ch.t_inv, ch.err)).astype(BF16)
        del ch.err

    def stage_predict(ch):
        eg_b = jnp.exp(ch.gc_b)
        kbg = ch.tile(1) * (ch.beta_b * eg_b)
        qd = ch.tile(0) * (q_scale * eg_b)
        ps = _mm(jnp.concatenate([kbg, qd], axis=0), st_ref[ch.d, ch.h])
        ch.resid = ch.tile(2) * ch.beta_b - ps[:CHUNK]
        ch.qs = ps[CHUNK:]
        del ch.beta_b

    def stage_solve(ch):
        ch.v_new = _mm(ch.t_inv, ch.resid).astype(BF16)
        del ch.resid, ch.t_inv

    def stage_update(ch):
        g_row = ch.g_row()
        g_end = g_row[:, CHUNK - 1:CHUNK] if ch.forward else g_row[:, 0:1]
        ch.o_ref[pl.ds(ch.s0, CHUNK), ch.lo:ch.lo + HEAD_DIM] = (
            ch.qs + _mm(ch.attn, ch.v_new)).astype(ch.o_ref.dtype)
        k_dec = ch.tile(1) * jnp.exp(g_end - ch.gc_b)
        st_ref[ch.d, ch.h] = st_ref[ch.d, ch.h] * jnp.exp(g_end) + _mm(k_dec, ch.v_new, TN_DIMS)

    stages = ([stage_gram, stage_power0] + [stage_power] * 4
              + [stage_power_last, stage_newton_residual, stage_newton_apply,
                 stage_predict, stage_solve, stage_update])

    def body(c, carry):
        cb = n_pos - 1 - c
        chains = []
        for h in range(N_HEADS):
            chains.append(Chain(xf_ref, fcf_ref, frf_ref, of_ref, c, h, True))
            chains.append(Chain(xb_ref, fcb_ref, frb_ref, ob_ref, cb, h, False))
        for stage in stages:
            for ch in chains:
                stage(ch)
        return carry

    lax.fori_loop(0, n_pos, body, 0)


def _delta_call(qkv, fc, fr, init, *, lb):
    b, l, _ = qkv.shape
    nb = l // lb
    n_pos = lb // CHUNK
    fwd3 = lambda i, j: (i, j, 0)
    bwd3 = lambda i, j: (i, nb - 1 - j, 0)
    st_spec = pl.BlockSpec((None, 2, N_HEADS, HEAD_DIM, HEAD_DIM), lambda i, j: (i, 0, 0, 0, 0))
    return pl.pallas_call(
        functools.partial(_delta_kernel, n_pos=n_pos),
        grid=(b, nb),
        in_specs=[
            pl.BlockSpec((None, lb, V_END), fwd3),
            pl.BlockSpec((None, lb, V_END), bwd3),
            pl.BlockSpec((None, lb, LANES), fwd3),
            pl.BlockSpec((None, lb, LANES), bwd3),
            pl.BlockSpec((None, n_pos, N_FEAT, CHUNK), lambda i, j: (i, j, 0, 0)),
            pl.BlockSpec((None, n_pos, N_FEAT, CHUNK), lambda i, j: (i, nb - 1 - j, 0, 0)),
            st_spec,
        ],
        out_specs=[
            pl.BlockSpec((None, lb, D_DELTA), fwd3),
            pl.BlockSpec((None, lb, D_DELTA), bwd3),
            st_spec,
        ],
        out_shape=[
            jax.ShapeDtypeStruct((b, l, D_DELTA), BF16),
            jax.ShapeDtypeStruct((b, l, D_DELTA), BF16),
            jax.ShapeDtypeStruct((b, 2, N_HEADS, HEAD_DIM, HEAD_DIM), F32),
        ],
        compiler_params=pltpu.CompilerParams(
            dimension_semantics=("arbitrary", "arbitrary"), vmem_limit_bytes=VMEM_LIMIT),
        name="delta",
    )(qkv, qkv, fc, fc, fr, fr, init)


CONF_PAD = 16
CONF_CCHUNK = 256
SUBLANES = 8
CONF_SHIFT_ROWS = GRID_W + 2 * CONF_PAD - SUBLANES
CONF_UNROLL = 4


def _conformer_kernel(a_ref, g_ref, dw_ref, lnw_ref, lnb_ref, shift_ref, o_ref, pad_ref, sh_ref, cv_ref, *,
                      n_seg):
    zeros = jnp.zeros((CONF_PAD, D_CONV), F32)
    for slot in range(CONF_UNROLL):
        pad_ref[slot, 0:CONF_PAD, :] = zeros
        pad_ref[slot, CONF_PAD + GRID_W:2 * CONF_PAD + GRID_W, :] = zeros
    lnw = lnw_ref[...]
    lnb = lnb_ref[...]

    def realign(slot, r0):
        pad_ref[slot, CONF_PAD:CONF_PAD + GRID_W, :] = (
            a_ref[pl.ds(r0, GRID_W), :] * jax.nn.sigmoid(g_ref[pl.ds(r0, GRID_W), :]))
        padded = pad_ref[slot]
        hi = padded.astype(BF16)
        lo = (padded - hi.astype(F32)).astype(BF16)
        sh_ref[slot] = jnp.dot(shift_ref[...], jnp.concatenate([hi, lo], axis=0),
                               preferred_element_type=F32)

    def taps(slot, r0):
        for c0 in range(0, D_CONV, CONF_CCHUNK):
            cols = slice(c0, c0 + CONF_CCHUNK)
            acc = None
            for tap in range(CONF_W):
                start = CONF_PAD - CONF_W // 2 + tap
                r = start % SUBLANES
                base = start - r
                if r == 0:
                    win = pad_ref[slot, base:base + GRID_W, cols]
                else:
                    row = (r - 1) * CONF_SHIFT_ROWS + base
                    win = sh_ref[slot, row:row + GRID_W, cols]
                term = win * dw_ref[tap:tap + 1, cols]
                acc = term if acc is None else acc + term
            cv_ref[:, cols] = acc
        u = cv_ref[...]
        mu = jnp.mean(u, axis=-1, keepdims=True)
        uc = u - mu
        var = jnp.mean(uc * uc, axis=-1, keepdims=True)
        y = uc * lax.rsqrt(var + EPS) * lnw + lnb
        o_ref[pl.ds(r0, GRID_W), :] = _silu(y).astype(o_ref.dtype)

    def body(it, carry):
        starts = [pl.multiple_of((it * CONF_UNROLL + u) * GRID_W, GRID_W) for u in range(CONF_UNROLL)]
        for slot, r0 in enumerate(starts):
            realign(slot, r0)
        for slot, r0 in enumerate(starts):
            taps(slot, r0)
        return carry

    lax.fori_loop(0, n_seg // CONF_UNROLL, body, 0)


def _conformer_shift_matrix():
    padded_rows = GRID_W + 2 * CONF_PAD
    r = jnp.arange(1, SUBLANES)[:, None, None]
    i = jnp.arange(CONF_SHIFT_ROWS)[None, :, None]
    col = jnp.arange(2 * padded_rows)[None, None, :]
    hit = (col == i + r) | (col == padded_rows + i + r)
    return hit.reshape((SUBLANES - 1) * CONF_SHIFT_ROWS, 2 * padded_rows).astype(BF16)


def _conformer_call(p2, dw, lnw, lnb, *, tt):
    m = p2.shape[0]
    a_blk = (V_END + D_DELTA) // D_CONV
    shift = _conformer_shift_matrix()
    return pl.pallas_call(
        functools.partial(_conformer_kernel, n_seg=tt // GRID_W),
        grid=(m // tt,),
        in_specs=[
            pl.BlockSpec((tt, D_CONV), lambda i: (i, a_blk)),
            pl.BlockSpec((tt, D_CONV), lambda i: (i, a_blk + 1)),
            pl.BlockSpec((CONF_W, D_CONV), lambda i: (0, 0)),
            pl.BlockSpec((1, D_CONV), lambda i: (0, 0)),
            pl.BlockSpec((1, D_CONV), lambda i: (0, 0)),
            pl.BlockSpec(shift.shape, lambda i: (0, 0)),
        ],
        out_specs=pl.BlockSpec((tt, D_CONV), lambda i: (i, 0)),
        out_shape=jax.ShapeDtypeStruct((m, D_CONV), BF16),
        scratch_shapes=[
            pltpu.VMEM((CONF_UNROLL, GRID_W + 2 * CONF_PAD, D_CONV), F32),
            pltpu.VMEM((CONF_UNROLL, (SUBLANES - 1) * CONF_SHIFT_ROWS, D_CONV), F32),
            pltpu.VMEM((GRID_W, D_CONV), F32),
        ],
        compiler_params=pltpu.CompilerParams(
            dimension_semantics=("arbitrary",), vmem_limit_bytes=VMEM_LIMIT),
        name="conformer",
    )(p2, p2, dw, lnw, lnb, shift)


def _outproj_kernel(of_ref, ob_ref, z_ref, dnw_ref, mc_ref, w1_ref, w2_ref, x_ref, g_ref, nw_ref, o_ref, *,
                    tiles_per_batch):
    b = pl.program_id(0) // tiles_per_batch
    dnw = dnw_ref[...]
    heads = []
    for h in range(N_HEADS):
        lo = h * HEAD_DIM
        oh = of_ref[:, lo:lo + HEAD_DIM].astype(F32) + ob_ref[:, lo:lo + HEAD_DIM].astype(F32)
        heads.append((_rms(oh, dnw) * _silu(z_ref[:, lo:lo + HEAD_DIM])).astype(BF16))
    md = jnp.concatenate(heads, axis=1)
    y = (jnp.dot(md, w1_ref[...], preferred_element_type=F32)
         + jnp.dot(mc_ref[...], w2_ref[...], preferred_element_type=F32))
    g = g_ref[pl.ds(b, 1), :]
    o_ref[...] = x_ref[...] + g * _rms(y, nw_ref[...])


def _outproj_call(o_f, o_b, p2, dnw, mc, w_out, x2, mod, norm_w, *, tm, tiles_per_batch):
    m = x2.shape[0]
    z_blk = V_END // D_DELTA
    return pl.pallas_call(
        functools.partial(_outproj_kernel, tiles_per_batch=tiles_per_batch),
        grid=(m // tm,),
        in_specs=[
            pl.BlockSpec((tm, D_DELTA), lambda i: (i, 0)),
            pl.BlockSpec((tm, D_DELTA), lambda i: (i, 0)),
            pl.BlockSpec((tm, D_DELTA), lambda i: (i, z_blk)),
            pl.BlockSpec((1, HEAD_DIM), lambda i: (0, 0)),
            pl.BlockSpec((tm, D_CONV), lambda i: (i, 0)),
            pl.BlockSpec((D_DELTA, D_MODEL), lambda i: (0, 0)),
            pl.BlockSpec((D_CONV, D_MODEL), lambda i: (1, 0)),
            pl.BlockSpec((tm, D_MODEL), lambda i: (i, 0)),
            pl.BlockSpec((8, D_MODEL), lambda i: (0, 2)),
            pl.BlockSpec((1, D_MODEL), lambda i: (0, 0)),
        ],
        out_specs=pl.BlockSpec((tm, D_MODEL), lambda i: (i, 0)),
        out_shape=jax.ShapeDtypeStruct((m, D_MODEL), F32),
        compiler_params=pltpu.CompilerParams(
            dimension_semantics=("arbitrary",), vmem_limit_bytes=VMEM_LIMIT),
        name="out_proj",
    )(o_f, o_b, p2, dnw, mc, w_out, w_out, x2, mod, norm_w)


def _ffn_kernel(x_ref, nw_ref, sh_ref, sc_ref, g_ref, pw_ref, wg_ref, wu_ref, wd_ref, o_ref, h_ref, *,
                tiles_per_batch, n_k):
    i = pl.program_id(0)
    k = pl.program_id(1)
    b = i // tiles_per_batch

    @pl.when(k == 0)
    def _():
        sh = sh_ref[pl.ds(b, 1), :]
        sc = sc_ref[pl.ds(b, 1), :]
        h_ref[...] = (_rms(x_ref[...], nw_ref[...]) * (1.0 + sc) + sh).astype(BF16)
        o_ref[...] = jnp.zeros(o_ref.shape, F32)

    hh = h_ref[...]
    gate = jnp.dot(hh, wg_ref[...], preferred_element_type=F32)
    up = jnp.dot(hh, wu_ref[...], preferred_element_type=F32)
    act = (_silu(gate) * up).astype(BF16)
    o_ref[...] += jnp.dot(act, wd_ref[...], preferred_element_type=F32)

    @pl.when(k == n_k - 1)
    def _():
        g = g_ref[pl.ds(b, 1), :]
        o_ref[...] = x_ref[...] + g * _rms(o_ref[...], pw_ref[...])


def _ffn_call(x2, mod, norm_pre, norm_post, wg, wu, wd, *, tm, tf, tiles_per_batch):
    m = x2.shape[0]
    dff = wg.shape[1]
    n_k = dff // tf
    return pl.pallas_call(
        functools.partial(_ffn_kernel, tiles_per_batch=tiles_per_batch, n_k=n_k),
        grid=(m // tm, n_k),
        in_specs=[
            pl.BlockSpec((tm, D_MODEL), lambda i, k: (i, 0)),
            pl.BlockSpec((1, D_MODEL), lambda i, k: (0, 0)),
            pl.BlockSpec((8, D_MODEL), lambda i, k: (0, 3)),
            pl.BlockSpec((8, D_MODEL), lambda i, k: (0, 4)),
            pl.BlockSpec((8, D_MODEL), lambda i, k: (0, 5)),
            pl.BlockSpec((1, D_MODEL), lambda i, k: (0, 0)),
            pl.BlockSpec((D_MODEL, tf), lambda i, k: (0, k)),
            pl.BlockSpec((D_MODEL, tf), lambda i, k: (0, k)),
            pl.BlockSpec((tf, D_MODEL), lambda i, k: (k, 0)),
        ],
        out_specs=pl.BlockSpec((tm, D_MODEL), lambda i, k: (i, 0)),
        out_shape=jax.ShapeDtypeStruct((m, D_MODEL), F32),
        scratch_shapes=[pltpu.VMEM((tm, D_MODEL), BF16)],
        compiler_params=pltpu.CompilerParams(
            dimension_semantics=("arbitrary", "arbitrary"), vmem_limit_bytes=VMEM_LIMIT),
        name="ffn",
    )(x2, norm_pre, mod, mod, mod, norm_post, wg, wu, wd)


@jax.jit
def _forward(x, c, ctx, c_ctx, w_mod, b_mod, mix_norm_pre, mix_norm_post, w_in, qkv_conv,
             a_log, dt_bias, delta_out_norm, conf_dw, conf_ln_w, conf_ln_b, w_out,
             ffn_norm_pre, ffn_norm_post, w_gate, w_up, w_down):
    bsz, seq, d = x.shape
    ctx_len = ctx.shape[1]
    assert d == D_MODEL and w_mod.shape[0] == 1, "single-layer kernel"
    assert seq % 1024 == 0 and ctx_len % (CHUNK * GATE_UNROLL) == 0 and bsz + 1 <= 8

    cc = jnp.zeros((8, D_MODEL), F32).at[:bsz].set(c).at[bsz].set(c_ctx)
    mod = _mod_call(cc, w_mod[0], b_mod[0][None, :])

    w_in0 = w_in[0]
    w_main = jnp.concatenate([w_in0[:, :V_END].astype(BF16), w_in0[:, STATE_END:].astype(BF16)], axis=1)
    w_g = jnp.zeros((D_MODEL, LANES), BF16).at[:, :N_FEAT].set(w_in0[:, V_END:STATE_END].astype(BF16))
    prm = (jnp.zeros((8, LANES), F32)
           .at[0, 2 * N_HEADS:N_FEAT].set(a_log[0].reshape(-1))
           .at[1, 2 * N_HEADS:N_FEAT].set(dt_bias[0].reshape(-1)))

    x2 = x.reshape(bsz * seq, D_MODEL)
    ctx2 = ctx.reshape(bsz * ctx_len, D_MODEL)
    tm = 1024
    p, ba = _inproj_call(x2, mix_norm_pre, mod, w_main, w_g, n_cols=D_MAIN, tm=tm, tn=1536,
                         tiles_per_mod_row=seq // tm, mod_row0=0)
    pc, bac = _inproj_call(ctx2, mix_norm_pre, mod, w_main, w_g, n_cols=V_END, tm=bsz * ctx_len, tn=1536,
                           tiles_per_mod_row=1, mod_row0=bsz)

    fc, fr = _gatefeat_call(ba.reshape(bsz, seq, LANES), prm)
    fc_c, fr_c = _gatefeat_call(bac.reshape(bsz, ctx_len, LANES), prm)
    qkv = _qkvconv_call(p.reshape(bsz, seq, D_MAIN), qkv_conv[0], tile=256)
    qkv_c = _qkvconv_call(pc.reshape(bsz, ctx_len, V_END), qkv_conv[0], tile=256)

    zero_state = jnp.zeros((bsz, 2, N_HEADS, HEAD_DIM, HEAD_DIM), F32)
    _, _, s_ctx = _delta_call(qkv_c, fc_c, fr_c, zero_state, lb=ctx_len)
    o_f, o_b, _ = _delta_call(qkv, fc, fr, s_ctx, lb=512)

    mix_c = _conformer_call(p, conf_dw[0], conf_ln_w, conf_ln_b, tt=512)

    x1 = _outproj_call(o_f.reshape(bsz * seq, D_DELTA), o_b.reshape(bsz * seq, D_DELTA), p, delta_out_norm,
                       mix_c, w_out[0].astype(BF16), x2, mod, mix_norm_post,
                       tm=512, tiles_per_batch=seq // 512)
    out = _ffn_call(x1, mod, ffn_norm_pre, ffn_norm_post, w_gate[0].astype(BF16), w_up[0].astype(BF16),
                    w_down[0].astype(BF16), tm=512, tf=512, tiles_per_batch=seq // 512)
    return out.reshape(bsz, seq, D_MODEL)


def kernel(x, c, ctx, c_ctx, w_mod, b_mod, mix_norm_pre, mix_norm_post, w_in, qkv_conv, a_log, dt_bias,
           delta_out_norm, conf_dw, conf_ln_w, conf_ln_b, w_out, ffn_norm_pre, ffn_norm_post,
           w_gate, w_up, w_down):
    return _forward(x, c, ctx, c_ctx, w_mod, b_mod, mix_norm_pre, mix_norm_post, w_in, qkv_conv,
                    a_log, dt_bias, delta_out_norm, conf_dw, conf_ln_w, conf_ln_b, w_out,
                    ffn_norm_pre, ffn_norm_post, w_gate, w_up, w_down)
```

```python
import functools
import math

import jax
import jax.numpy as jnp
from jax import lax
from jax.experimental import pallas as pl
from jax.experimental.pallas import tpu as pltpu

F32 = jnp.float32
BF16 = jnp.bfloat16
HIGHEST = lax.Precision.HIGHEST

D_MODEL = 2048
N_HEADS = 8
HEAD_DIM = 128
D_DELTA = N_HEADS * HEAD_DIM
D_CONV = D_MODEL - D_DELTA
GRID_W = 64
CHUNK = 64
SHORT_W = 5
CONF_W = 31
EPS = 1e-6
V_END = 3 * D_DELTA
STATE_END = V_END + 4 * N_HEADS
Z_END = STATE_END + D_DELTA
D_MAIN = V_END + D_DELTA + 2 * D_CONV
N_FEAT = 4 * N_HEADS
LANES = 128
VMEM_LIMIT = 56 * 1024 * 1024

NT_DIMS = (((1,), (1,)), ((), ()))
TN_DIMS = (((0,), (0,)), ((), ()))


def _mm(a, b, dims=None):
    a = a.astype(BF16)
    b = b.astype(BF16)
    if dims is None:
        return jnp.dot(a, b, preferred_element_type=F32)
    return lax.dot_general(a, b, dims, preferred_element_type=F32)


def _silu(x):
    return x * jax.nn.sigmoid(x)


def _rms(x, w):
    return x * lax.rsqrt(jnp.mean(x * x, axis=-1, keepdims=True) + EPS) * w


def _mod_kernel(c_ref, w_ref, b_ref, o_ref):
    s = _silu(c_ref[...])
    s1 = s.astype(BF16)
    s2 = (s - s1.astype(F32)).astype(BF16)
    w = w_ref[...]
    w_hi = w.astype(BF16)
    w_lo = (w - w_hi.astype(F32)).astype(BF16)
    rows = s.shape[0]
    p_hi = jnp.dot(jnp.concatenate([s1, s2], axis=0), w_hi, preferred_element_type=F32)
    p_lo = jnp.dot(s1, w_lo, preferred_element_type=F32)
    o_ref[...] = p_hi[:rows] + p_hi[rows:] + p_lo + b_ref[...]


def _mod_call(cc, w_mod, b_mod):
    n = w_mod.shape[1]
    tn = 1024
    return pl.pallas_call(
        _mod_kernel,
        grid=(n // tn,),
        in_specs=[
            pl.BlockSpec((8, D_MODEL), lambda j: (0, 0)),
            pl.BlockSpec((D_MODEL, tn), lambda j: (0, j)),
            pl.BlockSpec((1, tn), lambda j: (0, j)),
        ],
        out_specs=pl.BlockSpec((8, tn), lambda j: (0, j)),
        out_shape=jax.ShapeDtypeStruct((8, n), F32),
        compiler_params=pltpu.CompilerParams(
            dimension_semantics=("arbitrary",), vmem_limit_bytes=VMEM_LIMIT),
        name="mod",
    )(cc, w_mod, b_mod)


def _inproj_kernel(x_ref, nw_ref, sh_ref, sc_ref, w_ref, wg_ref, o_ref, og_ref, h_ref, *,
                   tiles_per_mod_row, mod_row0):
    i = pl.program_id(0)
    j = pl.program_id(1)

    @pl.when(j == 0)
    def _():
        r = mod_row0 + i // tiles_per_mod_row
        sh = sh_ref[pl.ds(r, 1), :]
        sc = sc_ref[pl.ds(r, 1), :]
        hh = (_rms(x_ref[...], nw_ref[...]) * (1.0 + sc) + sh).astype(BF16)
        h_ref[...] = hh
        og_ref[...] = jnp.dot(hh, wg_ref[...], preferred_element_type=F32)

    o_ref[...] = jnp.dot(h_ref[...], w_ref[...], preferred_element_type=F32)


def _inproj_call(x2, norm_w, mod, w_main, w_g, *, n_cols, tm, tn, tiles_per_mod_row, mod_row0):
    m = x2.shape[0]
    kern = functools.partial(_inproj_kernel, tiles_per_mod_row=tiles_per_mod_row, mod_row0=mod_row0)
    return pl.pallas_call(
        kern,
        grid=(m // tm, n_cols // tn),
        in_specs=[
            pl.BlockSpec((tm, D_MODEL), lambda i, j: (i, 0)),
            pl.BlockSpec((1, D_MODEL), lambda i, j: (0, 0)),
            pl.BlockSpec((8, D_MODEL), lambda i, j: (0, 0)),
            pl.BlockSpec((8, D_MODEL), lambda i, j: (0, 1)),
            pl.BlockSpec((D_MODEL, tn), lambda i, j: (0, j)),
            pl.BlockSpec((D_MODEL, LANES), lambda i, j: (0, 0)),
        ],
        out_specs=[
            pl.BlockSpec((tm, tn), lambda i, j: (i, j)),
            pl.BlockSpec((tm, LANES), lambda i, j: (i, 0)),
        ],
        out_shape=[
            jax.ShapeDtypeStruct((m, n_cols), F32),
            jax.ShapeDtypeStruct((m, LANES), F32),
        ],
        scratch_shapes=[pltpu.VMEM((tm, D_MODEL), BF16)],
        compiler_params=pltpu.CompilerParams(
            dimension_semantics=("arbitrary", "arbitrary"), vmem_limit_bytes=VMEM_LIMIT),
        name="in_proj",
    )(x2, norm_w, mod, mod, w_main, w_g)


GATE_UNROLL_MAX = 8


def _gatefeat_kernel(ba_ref, prm_ref, fc_ref, fr_ref, *, n_chunks, unroll):
    lane = lax.broadcasted_iota(jnp.int32, (CHUNK, LANES), 1)
    rr = lax.broadcasted_iota(jnp.int32, (CHUNK, CHUNK), 0)
    cc = lax.broadcasted_iota(jnp.int32, (CHUNK, CHUNK), 1)
    ltri = (rr >= cc).astype(F32)
    utri = (rr <= cc).astype(F32)
    r2 = lax.broadcasted_iota(jnp.int32, (LANES, LANES), 0)
    c2 = lax.broadcasted_iota(jnp.int32, (LANES, LANES), 1)
    eye = (r2 == c2).astype(F32)
    neg_a = -jnp.exp(prm_ref[0:1, :])
    dtb = prm_ref[1:2, :]

    def body(it, carry):
        idx = [it * unroll + u for u in range(unroll)]
        starts = [pl.multiple_of(ci * CHUNK, CHUNK) for ci in idx]
        xs = [ba_ref[pl.ds(s0, CHUNK), :] for s0 in starts]
        gs = []
        for x in xs:
            y = x + dtb
            gs.append(neg_a * (jnp.maximum(y, 0.0) + jnp.log1p(jnp.exp(-jnp.abs(y)))))
        pres = [jnp.dot(ltri, g, preferred_element_type=F32, precision=HIGHEST) for g in gs]
        sufs = [jnp.dot(utri, g, preferred_element_type=F32, precision=HIGHEST) for g in gs]
        fs = [jnp.where(lane < 2 * N_HEADS, jax.nn.sigmoid(x), jnp.where(lane < 3 * N_HEADS, pre, suf))
              for x, pre, suf in zip(xs, pres, sufs)]
        fts = [lax.dot_general(eye, f, NT_DIMS, preferred_element_type=F32, precision=HIGHEST) for f in fs]
        for ci, s0, f, ft in zip(idx, starts, fs, fts):
            fc_ref[pl.ds(s0, CHUNK), :] = f
            fr_ref[ci] = ft[0:N_FEAT, :]
        return carry

    lax.fori_loop(0, n_chunks // unroll, body, 0)


def _gatefeat_call(ba, prm):
    b, l, _ = ba.shape
    n_chunks = l // CHUNK
    unroll = math.gcd(n_chunks, GATE_UNROLL_MAX)
    return pl.pallas_call(
        functools.partial(_gatefeat_kernel, n_chunks=n_chunks, unroll=unroll),
        grid=(b,),
        in_specs=[
            pl.BlockSpec((None, l, LANES), lambda i: (i, 0, 0)),
            pl.BlockSpec((8, LANES), lambda i: (0, 0)),
        ],
        out_specs=[
            pl.BlockSpec((None, l, LANES), lambda i: (i, 0, 0)),
            pl.BlockSpec((None, n_chunks, N_FEAT, CHUNK), lambda i: (i, 0, 0, 0)),
        ],
        out_shape=[
            jax.ShapeDtypeStruct((b, l, LANES), F32),
            jax.ShapeDtypeStruct((b, n_chunks, N_FEAT, CHUNK), F32),
        ],
        compiler_params=pltpu.CompilerParams(
            dimension_semantics=("arbitrary",), vmem_limit_bytes=VMEM_LIMIT),
        name="gate_feat",
    )(ba, prm)


QKV_PAD = 8


def _qkvconv_kernel(x_ref, w_ref, o_ref, pad_ref, *, seq, tile, heads_per_step):
    j = pl.program_id(1)
    width = heads_per_step * HEAD_DIM
    zeros = jnp.zeros((QKV_PAD, width), F32)
    pad_ref[0:QKV_PAD, :] = zeros
    pad_ref[QKV_PAD + seq:2 * QKV_PAD + seq, :] = zeros
    pad_ref[QKV_PAD:QKV_PAD + seq, :] = x_ref[...]
    w = w_ref[...]
    for s in range(heads_per_step):
        cols = slice(s * HEAD_DIM, (s + 1) * HEAD_DIM)
        is_qk = j * heads_per_step + s < 2 * N_HEADS
        for t0 in range(0, seq, tile):
            acc = None
            for tap in range(SHORT_W):
                start = t0 + QKV_PAD - SHORT_W // 2 + tap
                term = pad_ref[start:start + tile, cols] * w[tap:tap + 1, cols]
                acc = term if acc is None else acc + term
            y = _silu(acc)
            inv = lax.rsqrt(jnp.sum(y * y, axis=-1, keepdims=True) + EPS)
            o_ref[t0:t0 + tile, cols] = (y * jnp.where(is_qk, inv, 1.0)).astype(o_ref.dtype)


def _qkvconv_call(p, conv_w, *, tile, heads_per_step):
    b, l, _ = p.shape
    width = heads_per_step * HEAD_DIM
    n_blocks = V_END // width
    return pl.pallas_call(
        functools.partial(_qkvconv_kernel, seq=l, tile=tile, heads_per_step=heads_per_step),
        grid=(b, n_blocks),
        in_specs=[
            pl.BlockSpec((None, l, width), lambda i, j: (i, 0, j)),
            pl.BlockSpec((SHORT_W, width), lambda i, j: (0, j)),
        ],
        out_specs=pl.BlockSpec((None, l, width), lambda i, j: (i, 0, j)),
        out_shape=jax.ShapeDtypeStruct((b, l, V_END), BF16),
        scratch_shapes=[pltpu.VMEM((l + 2 * QKV_PAD, width), F32)],
        compiler_params=pltpu.CompilerParams(
            dimension_semantics=("arbitrary", "arbitrary"), vmem_limit_bytes=VMEM_LIMIT),
        name="qkv_conv",
    )(p, conv_w)


def _delta_kernel(xf_ref, xb_ref, fcf_ref, fcb_ref, frf_ref, frb_ref, init_ref,
                  of_ref, ob_ref, st_ref, *, n_pos):
    @pl.when(pl.program_id(1) == 0)
    def _():
        st_ref[...] = init_ref[...]

    rr = lax.broadcasted_iota(jnp.int32, (CHUNK, CHUNK), 0)
    cc = lax.broadcasted_iota(jnp.int32, (CHUNK, CHUNK), 1)
    eye = (rr == cc).astype(F32)
    q_scale = HEAD_DIM ** -0.5

    class Chain:
        def __init__(self, x_ref, fc_ref, fr_ref, o_ref, c, h, forward):
            self.x_ref, self.fc_ref, self.fr_ref, self.o_ref = x_ref, fc_ref, fr_ref, o_ref
            self.c, self.h, self.forward = c, h, forward
            self.d = 0 if forward else 1
            self.s0 = pl.multiple_of(c * CHUNK, CHUNK)
            self.lo = h * HEAD_DIM

        def tile(self, which):
            off = which * D_DELTA + self.lo
            return self.x_ref[pl.ds(self.s0, CHUNK), off:off + HEAD_DIM].astype(F32)

        def cols(self):
            f = self.fc_ref[pl.ds(self.s0, CHUNK), :]
            ib = self.d * N_HEADS + self.h
            ig = (2 + self.d) * N_HEADS + self.h
            shape = (CHUNK, HEAD_DIM)
            return jnp.broadcast_to(f[:, ib:ib + 1], shape), jnp.broadcast_to(f[:, ig:ig + 1], shape)

        def g_row(self):
            ig = (2 + self.d) * N_HEADS + self.h
            return self.fr_ref[self.c, ig:ig + 1, :]

    def stage_gram(ch):
        ch.beta_b, ch.gc_b = ch.cols()
        k = ch.tile(1)
        ch.gram = _mm(jnp.concatenate([ch.tile(0) * q_scale, k * ch.beta_b], axis=0), k, NT_DIMS)
        incl = (rr >= cc) if ch.forward else (rr <= cc)
        ch.decay = jnp.where(incl, jnp.exp(jnp.where(incl, ch.gc_b[:, :CHUNK] - ch.g_row(), 0.0)), 0.0)

    def stage_power0(ch):
        strict = (rr > cc) if ch.forward else (rr < cc)
        ch.attn = (ch.gram[:CHUNK] * ch.decay).astype(BF16)
        neg = -jnp.where(strict, ch.gram[CHUNK:] * ch.decay, 0.0)
        ch.neg = neg.astype(BF16)
        ch.t_inv = eye + neg
        ch.pw = _mm(ch.neg, ch.neg)
        del ch.gram, ch.decay

    def stage_power(ch):
        both = _mm(jnp.concatenate([ch.pw, ch.t_inv], axis=0), ch.pw)
        ch.t_inv = ch.t_inv + both[CHUNK:]
        ch.pw = both[:CHUNK]

    def stage_power_last(ch):
        ch.t_inv = (ch.t_inv + _mm(ch.t_inv, ch.pw)).astype(BF16)
        del ch.pw

    def stage_newton_residual(ch):
        ch.err = eye - ch.t_inv.astype(F32) + _mm(ch.neg, ch.t_inv)
        del ch.neg

    def stage_newton_apply(ch):
        ch.t_inv = (ch.t_inv.astype(F32) + _mm(ch.t_inv, ch.err)).astype(BF16)
        del ch.err

    def stage_predict(ch):
        eg_b = jnp.exp(ch.gc_b)
        kbg = ch.tile(1) * (ch.beta_b * eg_b)
        qd = ch.tile(0) * (q_scale * eg_b)
        ps = _mm(jnp.concatenate([kbg, qd], axis=0), st_ref[ch.d, ch.h])
        ch.resid = ch.tile(2) * ch.beta_b - ps[:CHUNK]
        ch.qs = ps[CHUNK:]
        del ch.beta_b

    def stage_solve(ch):
        ch.v_new = _mm(ch.t_inv, ch.resid).astype(BF16)
        del ch.resid, ch.t_inv

    def stage_update(ch):
        g_row = ch.g_row()
        g_end = g_row[:, CHUNK - 1:CHUNK] if ch.forward else g_row[:, 0:1]
        ch.o_ref[pl.ds(ch.s0, CHUNK), ch.lo:ch.lo + HEAD_DIM] = (
            ch.qs + _mm(ch.attn, ch.v_new)).astype(ch.o_ref.dtype)
        k_dec = ch.tile(1) * jnp.exp(g_end - ch.gc_b)
        st_ref[ch.d, ch.h] = st_ref[ch.d, ch.h] * jnp.exp(g_end) + _mm(k_dec, ch.v_new, TN_DIMS)

    stages = ([stage_gram, stage_power0] + [stage_power] * 4
              + [stage_power_last, stage_newton_residual, stage_newton_apply,
                 stage_predict, stage_solve, stage_update])

    def body(c, carry):
        cb = n_pos - 1 - c
        chains = []
        for h in range(N_HEADS):
            chains.append(Chain(xf_ref, fcf_ref, frf_ref, of_ref, c, h, True))
            chains.append(Chain(xb_ref, fcb_ref, frb_ref, ob_ref, cb, h, False))
        for stage in stages:
            for ch in chains:
                stage(ch)
        return carry

    lax.fori_loop(0, n_pos, body, 0)


def _delta_call(qkv, fc, fr, init, *, lb):
    b, l, _ = qkv.shape
    nb = l // lb
    n_pos = lb // CHUNK
    fwd3 = lambda i, j: (i, j, 0)
    bwd3 = lambda i, j: (i, nb - 1 - j, 0)
    st_spec = pl.BlockSpec((None, 2, N_HEADS, HEAD_DIM, HEAD_DIM), lambda i, j: (i, 0, 0, 0, 0))
    return pl.pallas_call(
        functools.partial(_delta_kernel, n_pos=n_pos),
        grid=(b, nb),
        in_specs=[
            pl.BlockSpec((None, lb, V_END), fwd3),
            pl.BlockSpec((None, lb, V_END), bwd3),
            pl.BlockSpec((None, lb, LANES), fwd3),
            pl.BlockSpec((None, lb, LANES), bwd3),
            pl.BlockSpec((None, n_pos, N_FEAT, CHUNK), lambda i, j: (i, j, 0, 0)),
            pl.BlockSpec((None, n_pos, N_FEAT, CHUNK), lambda i, j: (i, nb - 1 - j, 0, 0)),
            st_spec,
        ],
        out_specs=[
            pl.BlockSpec((None, lb, D_DELTA), fwd3),
            pl.BlockSpec((None, lb, D_DELTA), bwd3),
            st_spec,
        ],
        out_shape=[
            jax.ShapeDtypeStruct((b, l, D_DELTA), BF16),
            jax.ShapeDtypeStruct((b, l, D_DELTA), BF16),
            jax.ShapeDtypeStruct((b, 2, N_HEADS, HEAD_DIM, HEAD_DIM), F32),
        ],
        compiler_params=pltpu.CompilerParams(
            dimension_semantics=("arbitrary", "arbitrary"), vmem_limit_bytes=VMEM_LIMIT),
        name="delta",
    )(qkv, qkv, fc, fc, fr, fr, init)


CONF_PAD = 16
CONF_CCHUNK = 256
SUBLANES = 8
CONF_SHIFT_ROWS = GRID_W + 2 * CONF_PAD - SUBLANES
CONF_UNROLL = 4


def _conformer_kernel(a_ref, g_ref, dw_ref, lnw_ref, lnb_ref, shift_ref, o_ref, pad_ref, sh_ref, cv_ref, *,
                      n_seg):
    zeros = jnp.zeros((CONF_PAD, D_CONV), F32)
    for slot in range(CONF_UNROLL):
        pad_ref[slot, 0:CONF_PAD, :] = zeros
        pad_ref[slot, CONF_PAD + GRID_W:2 * CONF_PAD + GRID_W, :] = zeros
    lnw = lnw_ref[...]
    lnb = lnb_ref[...]

    def realign(slot, r0):
        pad_ref[slot, CONF_PAD:CONF_PAD + GRID_W, :] = (
            a_ref[pl.ds(r0, GRID_W), :] * jax.nn.sigmoid(g_ref[pl.ds(r0, GRID_W), :]))
        padded = pad_ref[slot]
        hi = padded.astype(BF16)
        lo = (padded - hi.astype(F32)).astype(BF16)
        sh_ref[slot] = jnp.dot(shift_ref[...], jnp.concatenate([hi, lo], axis=0),
                               preferred_element_type=F32)

    def taps(slot, r0):
        for c0 in range(0, D_CONV, CONF_CCHUNK):
            cols = slice(c0, c0 + CONF_CCHUNK)
            acc = None
            for tap in range(CONF_W):
                start = CONF_PAD - CONF_W // 2 + tap
                r = start % SUBLANES
                base = start - r
                if r == 0:
                    win = pad_ref[slot, base:base + GRID_W, cols]
                else:
                    row = (r - 1) * CONF_SHIFT_ROWS + base
                    win = sh_ref[slot, row:row + GRID_W, cols]
                term = win * dw_ref[tap:tap + 1, cols]
                acc = term if acc is None else acc + term
            cv_ref[:, cols] = acc
        u = cv_ref[...]
        mu = jnp.mean(u, axis=-1, keepdims=True)
        uc = u - mu
        var = jnp.mean(uc * uc, axis=-1, keepdims=True)
        y = uc * lax.rsqrt(var + EPS) * lnw + lnb
        o_ref[pl.ds(r0, GRID_W), :] = _silu(y).astype(o_ref.dtype)

    def body(it, carry):
        starts = [pl.multiple_of((it * CONF_UNROLL + u) * GRID_W, GRID_W) for u in range(CONF_UNROLL)]
        for slot, r0 in enumerate(starts):
            realign(slot, r0)
        for slot, r0 in enumerate(starts):
            taps(slot, r0)
        return carry

    lax.fori_loop(0, n_seg // CONF_UNROLL, body, 0)


def _conformer_shift_matrix():
    padded_rows = GRID_W + 2 * CONF_PAD
    r = jnp.arange(1, SUBLANES)[:, None, None]
    i = jnp.arange(CONF_SHIFT_ROWS)[None, :, None]
    col = jnp.arange(2 * padded_rows)[None, None, :]
    hit = (col == i + r) | (col == padded_rows + i + r)
    return hit.reshape((SUBLANES - 1) * CONF_SHIFT_ROWS, 2 * padded_rows).astype(BF16)


def _conformer_call(p2, dw, lnw, lnb, *, tt):
    m = p2.shape[0]
    a_blk = (V_END + D_DELTA) // D_CONV
    shift = _conformer_shift_matrix()
    return pl.pallas_call(
        functools.partial(_conformer_kernel, n_seg=tt // GRID_W),
        grid=(m // tt,),
        in_specs=[
            pl.BlockSpec((tt, D_CONV), lambda i: (i, a_blk)),
            pl.BlockSpec((tt, D_CONV), lambda i: (i, a_blk + 1)),
            pl.BlockSpec((CONF_W, D_CONV), lambda i: (0, 0)),
            pl.BlockSpec((1, D_CONV), lambda i: (0, 0)),
            pl.BlockSpec((1, D_CONV), lambda i: (0, 0)),
            pl.BlockSpec(shift.shape, lambda i: (0, 0)),
        ],
        out_specs=pl.BlockSpec((tt, D_CONV), lambda i: (i, 0)),
        out_shape=jax.ShapeDtypeStruct((m, D_CONV), BF16),
        scratch_shapes=[
            pltpu.VMEM((CONF_UNROLL, GRID_W + 2 * CONF_PAD, D_CONV), F32),
            pltpu.VMEM((CONF_UNROLL, (SUBLANES - 1) * CONF_SHIFT_ROWS, D_CONV), F32),
            pltpu.VMEM((GRID_W, D_CONV), F32),
        ],
        compiler_params=pltpu.CompilerParams(
            dimension_semantics=("arbitrary",), vmem_limit_bytes=VMEM_LIMIT),
        name="conformer",
    )(p2, p2, dw, lnw, lnb, shift)


def _outproj_kernel(of_ref, ob_ref, z_ref, dnw_ref, mc_ref, w1_ref, w2_ref, x_ref, g_ref, nw_ref, o_ref, *,
                    tiles_per_batch):
    b = pl.program_id(0) // tiles_per_batch
    dnw = dnw_ref[...]
    heads = []
    for h in range(N_HEADS):
        lo = h * HEAD_DIM
        oh = of_ref[:, lo:lo + HEAD_DIM].astype(F32) + ob_ref[:, lo:lo + HEAD_DIM].astype(F32)
        heads.append((_rms(oh, dnw) * _silu(z_ref[:, lo:lo + HEAD_DIM])).astype(BF16))
    md = jnp.concatenate(heads, axis=1)
    y = (jnp.dot(md, w1_ref[...], preferred_element_type=F32)
         + jnp.dot(mc_ref[...], w2_ref[...], preferred_element_type=F32))
    g = g_ref[pl.ds(b, 1), :]
    o_ref[...] = x_ref[...] + g * _rms(y, nw_ref[...])


def _outproj_call(o_f, o_b, p2, dnw, mc, w_out, x2, mod, norm_w, *, tm, tiles_per_batch):
    m = x2.shape[0]
    z_blk = V_END // D_DELTA
    return pl.pallas_call(
        functools.partial(_outproj_kernel, tiles_per_batch=tiles_per_batch),
        grid=(m // tm,),
        in_specs=[
            pl.BlockSpec((tm, D_DELTA), lambda i: (i, 0)),
            pl.BlockSpec((tm, D_DELTA), lambda i: (i, 0)),
            pl.BlockSpec((tm, D_DELTA), lambda i: (i, z_blk)),
            pl.BlockSpec((1, HEAD_DIM), lambda i: (0, 0)),
            pl.BlockSpec((tm, D_CONV), lambda i: (i, 0)),
            pl.BlockSpec((D_DELTA, D_MODEL), lambda i: (0, 0)),
            pl.BlockSpec((D_CONV, D_MODEL), lambda i: (1, 0)),
            pl.BlockSpec((tm, D_MODEL), lambda i: (i, 0)),
            pl.BlockSpec((8, D_MODEL), lambda i: (0, 2)),
            pl.BlockSpec((1, D_MODEL), lambda i: (0, 0)),
        ],
        out_specs=pl.BlockSpec((tm, D_MODEL), lambda i: (i, 0)),
        out_shape=jax.ShapeDtypeStruct((m, D_MODEL), F32),
        compiler_params=pltpu.CompilerParams(
            dimension_semantics=("arbitrary",), vmem_limit_bytes=VMEM_LIMIT),
        name="out_proj",
    )(o_f, o_b, p2, dnw, mc, w_out, w_out, x2, mod, norm_w)


def _ffn_kernel(x_ref, nw_ref, sh_ref, sc_ref, g_ref, pw_ref, wg_ref, wu_ref, wd_ref, o_ref, h_ref, *,
                tiles_per_batch, n_k):
    i = pl.program_id(0)
    k = pl.program_id(1)
    b = i // tiles_per_batch

    @pl.when(k == 0)
    def _():
        sh = sh_ref[pl.ds(b, 1), :]
        sc = sc_ref[pl.ds(b, 1), :]
        h_ref[...] = (_rms(x_ref[...], nw_ref[...]) * (1.0 + sc) + sh).astype(BF16)
        o_ref[...] = jnp.zeros(o_ref.shape, F32)

    hh = h_ref[...]
    gate = jnp.dot(hh, wg_ref[...], preferred_element_type=F32)
    up = jnp.dot(hh, wu_ref[...], preferred_element_type=F32)
    act = (_silu(gate) * up).astype(BF16)
    o_ref[...] += jnp.dot(act, wd_ref[...], preferred_element_type=F32)

    @pl.when(k == n_k - 1)
    def _():
        g = g_ref[pl.ds(b, 1), :]
        o_ref[...] = x_ref[...] + g * _rms(o_ref[...], pw_ref[...])


def _ffn_call(x2, mod, norm_pre, norm_post, wg, wu, wd, *, tm, tf, tiles_per_batch):
    m = x2.shape[0]
    dff = wg.shape[1]
    n_k = dff // tf
    return pl.pallas_call(
        functools.partial(_ffn_kernel, tiles_per_batch=tiles_per_batch, n_k=n_k),
        grid=(m // tm, n_k),
        in_specs=[
            pl.BlockSpec((tm, D_MODEL), lambda i, k: (i, 0)),
            pl.BlockSpec((1, D_MODEL), lambda i, k: (0, 0)),
            pl.BlockSpec((8, D_MODEL), lambda i, k: (0, 3)),
            pl.BlockSpec((8, D_MODEL), lambda i, k: (0, 4)),
            pl.BlockSpec((8, D_MODEL), lambda i, k: (0, 5)),
            pl.BlockSpec((1, D_MODEL), lambda i, k: (0, 0)),
            pl.BlockSpec((D_MODEL, tf), lambda i, k: (0, k)),
            pl.BlockSpec((D_MODEL, tf), lambda i, k: (0, k)),
            pl.BlockSpec((tf, D_MODEL), lambda i, k: (k, 0)),
        ],
        out_specs=pl.BlockSpec((tm, D_MODEL), lambda i, k: (i, 0)),
        out_shape=jax.ShapeDtypeStruct((m, D_MODEL), F32),
        scratch_shapes=[pltpu.VMEM((tm, D_MODEL), BF16)],
        compiler_params=pltpu.CompilerParams(
            dimension_semantics=("arbitrary", "arbitrary"), vmem_limit_bytes=VMEM_LIMIT),
        name="ffn",
    )(x2, norm_pre, mod, mod, mod, norm_post, wg, wu, wd)


@jax.jit
def _forward(x, c, ctx, c_ctx, w_mod, b_mod, mix_norm_pre, mix_norm_post, w_in, qkv_conv,
             a_log, dt_bias, delta_out_norm, conf_dw, conf_ln_w, conf_ln_b, w_out,
             ffn_norm_pre, ffn_norm_post, w_gate, w_up, w_down):
    bsz, seq, d = x.shape
    ctx_len = ctx.shape[1]
    assert d == D_MODEL and w_mod.shape[0] == 1, "single-layer kernel"
    assert seq % 1024 == 0 and ctx_len % CHUNK == 0 and bsz + 1 <= 8

    cc = jnp.zeros((8, D_MODEL), F32).at[:bsz].set(c).at[bsz].set(c_ctx)
    mod = _mod_call(cc, w_mod[0], b_mod[0][None, :])

    w_in0 = w_in[0]
    w_main = jnp.concatenate([w_in0[:, :V_END].astype(BF16), w_in0[:, STATE_END:].astype(BF16)], axis=1)
    w_g = jnp.zeros((D_MODEL, LANES), BF16).at[:, :N_FEAT].set(w_in0[:, V_END:STATE_END].astype(BF16))
    prm = (jnp.zeros((8, LANES), F32)
           .at[0, 2 * N_HEADS:N_FEAT].set(a_log[0].reshape(-1))
           .at[1, 2 * N_HEADS:N_FEAT].set(dt_bias[0].reshape(-1)))

    x2 = x.reshape(bsz * seq, D_MODEL)
    ctx2 = ctx.reshape(bsz * ctx_len, D_MODEL)
    tm = 1024
    p, ba = _inproj_call(x2, mix_norm_pre, mod, w_main, w_g, n_cols=D_MAIN, tm=tm, tn=1536,
                         tiles_per_mod_row=seq // tm, mod_row0=0)
    pc, bac = _inproj_call(ctx2, mix_norm_pre, mod, w_main, w_g, n_cols=V_END, tm=bsz * ctx_len, tn=1536,
                           tiles_per_mod_row=1, mod_row0=bsz)

    fc, fr = _gatefeat_call(ba.reshape(bsz, seq, LANES), prm)
    fc_c, fr_c = _gatefeat_call(bac.reshape(bsz, ctx_len, LANES), prm)
    qkv = _qkvconv_call(p.reshape(bsz, seq, D_MAIN), qkv_conv[0], tile=256, heads_per_step=1)
    qkv_c = _qkvconv_call(pc.reshape(bsz, ctx_len, V_END), qkv_conv[0], tile=256, heads_per_step=N_HEADS)

    zero_state = jnp.zeros((bsz, 2, N_HEADS, HEAD_DIM, HEAD_DIM), F32)
    _, _, s_ctx = _delta_call(qkv_c, fc_c, fr_c, zero_state, lb=ctx_len)
    o_f, o_b, _ = _delta_call(qkv, fc, fr, s_ctx, lb=512)

    mix_c = _conformer_call(p, conf_dw[0], conf_ln_w, conf_ln_b, tt=512)

    x1 = _outproj_call(o_f.reshape(bsz * seq, D_DELTA), o_b.reshape(bsz * seq, D_DELTA), p, delta_out_norm,
                       mix_c, w_out[0].astype(BF16), x2, mod, mix_norm_post,
                       tm=512, tiles_per_batch=seq // 512)
    out = _ffn_call(x1, mod, ffn_norm_pre, ffn_norm_post, w_gate[0].astype(BF16), w_up[0].astype(BF16),
                    w_down[0].astype(BF16), tm=512, tf=512, tiles_per_batch=seq // 512)
    return out.reshape(bsz, seq, D_MODEL)


def kernel(x, c, ctx, c_ctx, w_mod, b_mod, mix_norm_pre, mix_norm_post, w_in, qkv_conv, a_log, dt_bias,
           delta_out_norm, conf_dw, conf_ln_w, conf_ln_b, w_out, ffn_norm_pre, ffn_norm_post,
           w_gate, w_up, w_down):
    return _forward(x, c, ctx, c_ctx, w_mod, b_mod, mix_norm_pre, mix_norm_post, w_in, qkv_conv,
                    a_log, dt_bias, delta_out_norm, conf_dw, conf_ln_w, conf_ln_b, w_out,
                    ffn_norm_pre, ffn_norm_post, w_gate, w_up, w_down)
```

```python
import functools
import math

import jax
import jax.numpy as jnp
from jax import lax
from jax.experimental import pallas as pl
from jax.experimental.pallas import tpu as pltpu

F32 = jnp.float32
BF16 = jnp.bfloat16
HIGHEST = lax.Precision.HIGHEST

D_MODEL = 2048
N_HEADS = 8
HEAD_DIM = 128
D_DELTA = N_HEADS * HEAD_DIM
D_CONV = D_MODEL - D_DELTA
GRID_W = 64
CHUNK = 64
SHORT_W = 5
CONF_W = 31
EPS = 1e-6
V_END = 3 * D_DELTA
STATE_END = V_END + 4 * N_HEADS
Z_END = STATE_END + D_DELTA
D_MAIN = V_END + D_DELTA + 2 * D_CONV
N_FEAT = 4 * N_HEADS
LANES = 128
VMEM_LIMIT = 56 * 1024 * 1024

NT_DIMS = (((1,), (1,)), ((), ()))
TN_DIMS = (((0,), (0,)), ((), ()))


def _mm(a, b, dims=None):
    a = a.astype(BF16)
    b = b.astype(BF16)
    if dims is None:
        return jnp.dot(a, b, preferred_element_type=F32)
    return lax.dot_general(a, b, dims, preferred_element_type=F32)


def _silu(x):
    return x * jax.nn.sigmoid(x)


def _rms(x, w):
    return x * lax.rsqrt(jnp.mean(x * x, axis=-1, keepdims=True) + EPS) * w


def _mod_kernel(c_ref, w_ref, b_ref, o_ref):
    s = _silu(c_ref[...])
    s1 = s.astype(BF16)
    s2 = (s - s1.astype(F32)).astype(BF16)
    w = w_ref[...]
    w_hi = w.astype(BF16)
    w_lo = (w - w_hi.astype(F32)).astype(BF16)
    rows = s.shape[0]
    p_hi = jnp.dot(jnp.concatenate([s1, s2], axis=0), w_hi, preferred_element_type=F32)
    p_lo = jnp.dot(s1, w_lo, preferred_element_type=F32)
    o_ref[...] = p_hi[:rows] + p_hi[rows:] + p_lo + b_ref[...]


def _mod_call(cc, w_mod, b_mod):
    n = w_mod.shape[1]
    tn = 1024
    return pl.pallas_call(
        _mod_kernel,
        grid=(n // tn,),
        in_specs=[
            pl.BlockSpec((8, D_MODEL), lambda j: (0, 0)),
            pl.BlockSpec((D_MODEL, tn), lambda j: (0, j)),
            pl.BlockSpec((1, tn), lambda j: (0, j)),
        ],
        out_specs=pl.BlockSpec((8, tn), lambda j: (0, j)),
        out_shape=jax.ShapeDtypeStruct((8, n), F32),
        compiler_params=pltpu.CompilerParams(
            dimension_semantics=("arbitrary",), vmem_limit_bytes=VMEM_LIMIT),
        name="mod",
    )(cc, w_mod, b_mod)


WPREP_TN = 512


def _wprep_kernel(a_ref, b_ref, o_ref, og_ref):
    j = pl.program_id(0)
    n_gate = STATE_END - V_END
    first_shifted = V_END // WPREP_TN

    @pl.when(j < first_shifted)
    def _():
        o_ref[...] = a_ref[...].astype(BF16)

    @pl.when(j >= first_shifted)
    def _():
        both = jnp.concatenate([a_ref[...], b_ref[:, :LANES]], axis=1)
        o_ref[...] = both[:, n_gate:n_gate + WPREP_TN].astype(BF16)

    @pl.when(j == first_shifted)
    def _():
        head = a_ref[:, :LANES]
        lane = lax.broadcasted_iota(jnp.int32, head.shape, 1)
        og_ref[...] = jnp.where(lane < n_gate, head, 0.0).astype(BF16)


def _wprep_call(w_in0):
    d, n_in = w_in0.shape
    assert n_in == D_MAIN + STATE_END - V_END and V_END % WPREP_TN == 0 and D_MAIN % WPREP_TN == 0
    return pl.pallas_call(
        _wprep_kernel,
        grid=(D_MAIN // WPREP_TN,),
        in_specs=[
            pl.BlockSpec((d, WPREP_TN), lambda j: (0, j)),
            pl.BlockSpec((d, WPREP_TN), lambda j: (0, j + 1)),
        ],
        out_specs=[
            pl.BlockSpec((d, WPREP_TN), lambda j: (0, j)),
            pl.BlockSpec((d, LANES), lambda j: (0, 0)),
        ],
        out_shape=[
            jax.ShapeDtypeStruct((d, D_MAIN), BF16),
            jax.ShapeDtypeStruct((d, LANES), BF16),
        ],
        compiler_params=pltpu.CompilerParams(
            dimension_semantics=("arbitrary",), vmem_limit_bytes=VMEM_LIMIT),
        name="w_prep",
    )(w_in0, w_in0)


def _inproj_kernel(x_ref, nw_ref, sh_ref, sc_ref, w_ref, wg_ref, o_ref, og_ref, h_ref, *,
                   tiles_per_mod_row, mod_row0):
    i = pl.program_id(0)
    j = pl.program_id(1)

    @pl.when(j == 0)
    def _():
        r = mod_row0 + i // tiles_per_mod_row
        sh = sh_ref[pl.ds(r, 1), :]
        sc = sc_ref[pl.ds(r, 1), :]
        hh = (_rms(x_ref[...], nw_ref[...]) * (1.0 + sc) + sh).astype(BF16)
        h_ref[...] = hh
        og_ref[...] = jnp.dot(hh, wg_ref[...], preferred_element_type=F32)

    o_ref[...] = jnp.dot(h_ref[...], w_ref[...], preferred_element_type=F32)


def _inproj_call(x2, norm_w, mod, w_main, w_g, *, n_cols, tm, tn, tiles_per_mod_row, mod_row0):
    m = x2.shape[0]
    kern = functools.partial(_inproj_kernel, tiles_per_mod_row=tiles_per_mod_row, mod_row0=mod_row0)
    return pl.pallas_call(
        kern,
        grid=(m // tm, n_cols // tn),
        in_specs=[
            pl.BlockSpec((tm, D_MODEL), lambda i, j: (i, 0)),
            pl.BlockSpec((1, D_MODEL), lambda i, j: (0, 0)),
            pl.BlockSpec((8, D_MODEL), lambda i, j: (0, 0)),
            pl.BlockSpec((8, D_MODEL), lambda i, j: (0, 1)),
            pl.BlockSpec((D_MODEL, tn), lambda i, j: (0, j)),
            pl.BlockSpec((D_MODEL, LANES), lambda i, j: (0, 0)),
        ],
        out_specs=[
            pl.BlockSpec((tm, tn), lambda i, j: (i, j)),
            pl.BlockSpec((tm, LANES), lambda i, j: (i, 0)),
        ],
        out_shape=[
            jax.ShapeDtypeStruct((m, n_cols), F32),
            jax.ShapeDtypeStruct((m, LANES), F32),
        ],
        scratch_shapes=[pltpu.VMEM((tm, D_MODEL), BF16)],
        compiler_params=pltpu.CompilerParams(
            dimension_semantics=("arbitrary", "arbitrary"), vmem_limit_bytes=VMEM_LIMIT),
        name="in_proj",
    )(x2, norm_w, mod, mod, w_main, w_g)


GATE_UNROLL_MAX = 8


def _gatefeat_kernel(ba_ref, prm_ref, fc_ref, fr_ref, *, n_chunks, unroll):
    lane = lax.broadcasted_iota(jnp.int32, (CHUNK, LANES), 1)
    rr = lax.broadcasted_iota(jnp.int32, (CHUNK, CHUNK), 0)
    cc = lax.broadcasted_iota(jnp.int32, (CHUNK, CHUNK), 1)
    ltri = (rr >= cc).astype(F32)
    utri = (rr <= cc).astype(F32)
    r2 = lax.broadcasted_iota(jnp.int32, (LANES, LANES), 0)
    c2 = lax.broadcasted_iota(jnp.int32, (LANES, LANES), 1)
    eye = (r2 == c2).astype(F32)
    neg_a = -jnp.exp(prm_ref[0:1, :])
    dtb = prm_ref[1:2, :]

    def body(it, carry):
        idx = [it * unroll + u for u in range(unroll)]
        starts = [pl.multiple_of(ci * CHUNK, CHUNK) for ci in idx]
        xs = [ba_ref[pl.ds(s0, CHUNK), :] for s0 in starts]
        gs = []
        for x in xs:
            y = x + dtb
            gs.append(neg_a * (jnp.maximum(y, 0.0) + jnp.log1p(jnp.exp(-jnp.abs(y)))))
        pres = [jnp.dot(ltri, g, preferred_element_type=F32, precision=HIGHEST) for g in gs]
        sufs = [jnp.dot(utri, g, preferred_element_type=F32, precision=HIGHEST) for g in gs]
        fs = [jnp.where(lane < 2 * N_HEADS, jax.nn.sigmoid(x), jnp.where(lane < 3 * N_HEADS, pre, suf))
              for x, pre, suf in zip(xs, pres, sufs)]
        fts = [lax.dot_general(eye, f, NT_DIMS, preferred_element_type=F32, precision=HIGHEST) for f in fs]
        for ci, s0, f, ft in zip(idx, starts, fs, fts):
            fc_ref[pl.ds(s0, CHUNK), :] = f
            fr_ref[ci] = ft[0:N_FEAT, :]
        return carry

    lax.fori_loop(0, n_chunks // unroll, body, 0)


def _gatefeat_call(ba, prm):
    b, l, _ = ba.shape
    n_chunks = l // CHUNK
    unroll = math.gcd(n_chunks, GATE_UNROLL_MAX)
    return pl.pallas_call(
        functools.partial(_gatefeat_kernel, n_chunks=n_chunks, unroll=unroll),
        grid=(b,),
        in_specs=[
            pl.BlockSpec((None, l, LANES), lambda i: (i, 0, 0)),
            pl.BlockSpec((8, LANES), lambda i: (0, 0)),
        ],
        out_specs=[
            pl.BlockSpec((None, l, LANES), lambda i: (i, 0, 0)),
            pl.BlockSpec((None, n_chunks, N_FEAT, CHUNK), lambda i: (i, 0, 0, 0)),
        ],
        out_shape=[
            jax.ShapeDtypeStruct((b, l, LANES), F32),
            jax.ShapeDtypeStruct((b, n_chunks, N_FEAT, CHUNK), F32),
        ],
        compiler_params=pltpu.CompilerParams(
            dimension_semantics=("arbitrary",), vmem_limit_bytes=VMEM_LIMIT),
        name="gate_feat",
    )(ba, prm)


QKV_PAD = 8


def _qkvconv_kernel(x_ref, w_ref, o_ref, pad_ref, *, seq, tile, heads_per_step):
    j = pl.program_id(1)
    width = heads_per_step * HEAD_DIM
    zeros = jnp.zeros((QKV_PAD, width), F32)
    pad_ref[0:QKV_PAD, :] = zeros
    pad_ref[QKV_PAD + seq:2 * QKV_PAD + seq, :] = zeros
    pad_ref[QKV_PAD:QKV_PAD + seq, :] = x_ref[...]
    w = w_ref[...]
    for s in range(heads_per_step):
        cols = slice(s * HEAD_DIM, (s + 1) * HEAD_DIM)
        is_qk = j * heads_per_step + s < 2 * N_HEADS
        for t0 in range(0, seq, tile):
            acc = None
            for tap in range(SHORT_W):
                start = t0 + QKV_PAD - SHORT_W // 2 + tap
                term = pad_ref[start:start + tile, cols] * w[tap:tap + 1, cols]
                acc = term if acc is None else acc + term
            y = _silu(acc)
            inv = lax.rsqrt(jnp.sum(y * y, axis=-1, keepdims=True) + EPS)
            o_ref[t0:t0 + tile, cols] = (y * jnp.where(is_qk, inv, 1.0)).astype(o_ref.dtype)


def _qkvconv_call(p, conv_w, *, tile, heads_per_step):
    b, l, _ = p.shape
    width = heads_per_step * HEAD_DIM
    n_blocks = V_END // width
    return pl.pallas_call(
        functools.partial(_qkvconv_kernel, seq=l, tile=tile, heads_per_step=heads_per_step),
        grid=(b, n_blocks),
        in_specs=[
            pl.BlockSpec((None, l, width), lambda i, j: (i, 0, j)),
            pl.BlockSpec((SHORT_W, width), lambda i, j: (0, j)),
        ],
        out_specs=pl.BlockSpec((None, l, width), lambda i, j: (i, 0, j)),
        out_shape=jax.ShapeDtypeStruct((b, l, V_END), BF16),
        scratch_shapes=[pltpu.VMEM((l + 2 * QKV_PAD, width), F32)],
        compiler_params=pltpu.CompilerParams(
            dimension_semantics=("arbitrary", "arbitrary"), vmem_limit_bytes=VMEM_LIMIT),
        name="qkv_conv",
    )(p, conv_w)


def _delta_kernel(xf_ref, xb_ref, fcf_ref, fcb_ref, frf_ref, frb_ref, init_ref,
                  of_ref, ob_ref, st_ref, *, n_pos):
    @pl.when(pl.program_id(1) == 0)
    def _():
        st_ref[...] = init_ref[...]

    rr = lax.broadcasted_iota(jnp.int32, (CHUNK, CHUNK), 0)
    cc = lax.broadcasted_iota(jnp.int32, (CHUNK, CHUNK), 1)
    eye = (rr == cc).astype(F32)
    q_scale = HEAD_DIM ** -0.5

    class Chain:
        def __init__(self, x_ref, fc_ref, fr_ref, o_ref, c, h, forward):
            self.x_ref, self.fc_ref, self.fr_ref, self.o_ref = x_ref, fc_ref, fr_ref, o_ref
            self.c, self.h, self.forward = c, h, forward
            self.d = 0 if forward else 1
            self.s0 = pl.multiple_of(c * CHUNK, CHUNK)
            self.lo = h * HEAD_DIM

        def tile(self, which):
            off = which * D_DELTA + self.lo
            return self.x_ref[pl.ds(self.s0, CHUNK), off:off + HEAD_DIM].astype(F32)

        def cols(self):
            f = self.fc_ref[pl.ds(self.s0, CHUNK), :]
            ib = self.d * N_HEADS + self.h
            ig = (2 + self.d) * N_HEADS + self.h
            shape = (CHUNK, HEAD_DIM)
            return jnp.broadcast_to(f[:, ib:ib + 1], shape), jnp.broadcast_to(f[:, ig:ig + 1], shape)

        def g_row(self):
            ig = (2 + self.d) * N_HEADS + self.h
            return self.fr_ref[self.c, ig:ig + 1, :]

    def stage_gram(ch):
        ch.beta_b, ch.gc_b = ch.cols()
        k = ch.tile(1)
        ch.gram = _mm(jnp.concatenate([ch.tile(0) * q_scale, k * ch.beta_b], axis=0), k, NT_DIMS)
        incl = (rr >= cc) if ch.forward else (rr <= cc)
        ch.decay = jnp.where(incl, jnp.exp(jnp.where(incl, ch.gc_b[:, :CHUNK] - ch.g_row(), 0.0)), 0.0)

    def stage_power0(ch):
        strict = (rr > cc) if ch.forward else (rr < cc)
        ch.attn = (ch.gram[:CHUNK] * ch.decay).astype(BF16)
        neg = -jnp.where(strict, ch.gram[CHUNK:] * ch.decay, 0.0)
        ch.neg = neg.astype(BF16)
        ch.t_inv = eye + neg
        ch.pw = _mm(ch.neg, ch.neg)
        del ch.gram, ch.decay

    def stage_power(ch):
        both = _mm(jnp.concatenate([ch.pw, ch.t_inv], axis=0), ch.pw)
        ch.t_inv = ch.t_inv + both[CHUNK:]
        ch.pw = both[:CHUNK]

    def stage_power_last(ch):
        ch.t_inv = (ch.t_inv + _mm(ch.t_inv, ch.pw)).astype(BF16)
        del ch.pw

    def stage_newton_residual(ch):
        ch.err = eye - ch.t_inv.astype(F32) + _mm(ch.neg, ch.t_inv)
        del ch.neg

    def stage_newton_apply(ch):
        ch.t_inv = (ch.t_inv.astype(F32) + _mm(ch.t_inv, ch.err)).astype(BF16)
        del ch.err

    def stage_predict(ch):
        eg_b = jnp.exp(ch.gc_b)
        kbg = ch.tile(1) * (ch.beta_b * eg_b)
        qd = ch.tile(0) * (q_scale * eg_b)
        ps = _mm(jnp.concatenate([kbg, qd], axis=0), st_ref[ch.d, ch.h])
        ch.resid = ch.tile(2) * ch.beta_b - ps[:CHUNK]
        ch.qs = ps[CHUNK:]
        del ch.beta_b

    def stage_solve(ch):
        ch.v_new = _mm(ch.t_inv, ch.resid).astype(BF16)
        del ch.resid, ch.t_inv

    def stage_update(ch):
        g_row = ch.g_row()
        g_end = g_row[:, CHUNK - 1:CHUNK] if ch.forward else g_row[:, 0:1]
        ch.o_ref[pl.ds(ch.s0, CHUNK), ch.lo:ch.lo + HEAD_DIM] = (
            ch.qs + _mm(ch.attn, ch.v_new)).astype(ch.o_ref.dtype)
        k_dec = ch.tile(1) * jnp.exp(g_end - ch.gc_b)
        st_ref[ch.d, ch.h] = st_ref[ch.d, ch.h] * jnp.exp(g_end) + _mm(k_dec, ch.v_new, TN_DIMS)

    stages = ([stage_gram, stage_power0] + [stage_power] * 4
              + [stage_power_last, stage_newton_residual, stage_newton_apply,
                 stage_predict, stage_solve, stage_update])

    def body(c, carry):
        cb = n_pos - 1 - c
        chains = []
        for h in range(N_HEADS):
            chains.append(Chain(xf_ref, fcf_ref, frf_ref, of_ref, c, h, True))
            chains.append(Chain(xb_ref, fcb_ref, frb_ref, ob_ref, cb, h, False))
        for stage in stages:
            for ch in chains:
                stage(ch)
        return carry

    lax.fori_loop(0, n_pos, body, 0)


def _delta_call(qkv, fc, fr, init, *, lb):
    b, l, _ = qkv.shape
    nb = l // lb
    n_pos = lb // CHUNK
    fwd3 = lambda i, j: (i, j, 0)
    bwd3 = lambda i, j: (i, nb - 1 - j, 0)
    st_spec = pl.BlockSpec((None, 2, N_HEADS, HEAD_DIM, HEAD_DIM), lambda i, j: (i, 0, 0, 0, 0))
    return pl.pallas_call(
        functools.partial(_delta_kernel, n_pos=n_pos),
        grid=(b, nb),
        in_specs=[
            pl.BlockSpec((None, lb, V_END), fwd3),
            pl.BlockSpec((None, lb, V_END), bwd3),
            pl.BlockSpec((None, lb, LANES), fwd3),
            pl.BlockSpec((None, lb, LANES), bwd3),
            pl.BlockSpec((None, n_pos, N_FEAT, CHUNK), lambda i, j: (i, j, 0, 0)),
            pl.BlockSpec((None, n_pos, N_FEAT, CHUNK), lambda i, j: (i, nb - 1 - j, 0, 0)),
            st_spec,
        ],
        out_specs=[
            pl.BlockSpec((None, lb, D_DELTA), fwd3),
            pl.BlockSpec((None, lb, D_DELTA), bwd3),
            st_spec,
        ],
        out_shape=[
            jax.ShapeDtypeStruct((b, l, D_DELTA), BF16),
            jax.ShapeDtypeStruct((b, l, D_DELTA), BF16),
            jax.ShapeDtypeStruct((b, 2, N_HEADS, HEAD_DIM, HEAD_DIM), F32),
        ],
        compiler_params=pltpu.CompilerParams(
            dimension_semantics=("arbitrary", "arbitrary"), vmem_limit_bytes=VMEM_LIMIT),
        name="delta",
    )(qkv, qkv, fc, fc, fr, fr, init)


CONF_PAD = 16
CONF_CCHUNK = 256
SUBLANES = 8
CONF_SHIFT_ROWS = GRID_W + 2 * CONF_PAD - SUBLANES
CONF_UNROLL = 4


def _conformer_kernel(a_ref, g_ref, dw_ref, lnw_ref, lnb_ref, shift_ref, o_ref, pad_ref, sh_ref, cv_ref, *,
                      n_seg):
    zeros = jnp.zeros((CONF_PAD, D_CONV), F32)
    for slot in range(CONF_UNROLL):
        pad_ref[slot, 0:CONF_PAD, :] = zeros
        pad_ref[slot, CONF_PAD + GRID_W:2 * CONF_PAD + GRID_W, :] = zeros
    lnw = lnw_ref[...]
    lnb = lnb_ref[...]

    def realign(slot, r0):
        pad_ref[slot, CONF_PAD:CONF_PAD + GRID_W, :] = (
            a_ref[pl.ds(r0, GRID_W), :] * jax.nn.sigmoid(g_ref[pl.ds(r0, GRID_W), :]))
        padded = pad_ref[slot]
        hi = padded.astype(BF16)
        lo = (padded - hi.astype(F32)).astype(BF16)
        sh_ref[slot] = jnp.dot(shift_ref[...], jnp.concatenate([hi, lo], axis=0),
                               preferred_element_type=F32)

    def taps(slot, r0):
        for c0 in range(0, D_CONV, CONF_CCHUNK):
            cols = slice(c0, c0 + CONF_CCHUNK)
            acc = None
            for tap in range(CONF_W):
                start = CONF_PAD - CONF_W // 2 + tap
                r = start % SUBLANES
                base = start - r
                if r == 0:
                    win = pad_ref[slot, base:base + GRID_W, cols]
                else:
                    row = (r - 1) * CONF_SHIFT_ROWS + base
                    win = sh_ref[slot, row:row + GRID_W, cols]
                term = win * dw_ref[tap:tap + 1, cols]
                acc = term if acc is None else acc + term
            cv_ref[:, cols] = acc
        u = cv_ref[...]
        mu = jnp.mean(u, axis=-1, keepdims=True)
        uc = u - mu
        var = jnp.mean(uc * uc, axis=-1, keepdims=True)
        y = uc * lax.rsqrt(var + EPS) * lnw + lnb
        o_ref[pl.ds(r0, GRID_W), :] = _silu(y).astype(o_ref.dtype)

    def body(it, carry):
        starts = [pl.multiple_of((it * CONF_UNROLL + u) * GRID_W, GRID_W) for u in range(CONF_UNROLL)]
        for slot, r0 in enumerate(starts):
            realign(slot, r0)
        for slot, r0 in enumerate(starts):
            taps(slot, r0)
        return carry

    lax.fori_loop(0, n_seg // CONF_UNROLL, body, 0)


def _conformer_shift_matrix():
    padded_rows = GRID_W + 2 * CONF_PAD
    r = jnp.arange(1, SUBLANES)[:, None, None]
    i = jnp.arange(CONF_SHIFT_ROWS)[None, :, None]
    col = jnp.arange(2 * padded_rows)[None, None, :]
    hit = (col == i + r) | (col == padded_rows + i + r)
    return hit.reshape((SUBLANES - 1) * CONF_SHIFT_ROWS, 2 * padded_rows).astype(BF16)


def _conformer_call(p2, dw, lnw, lnb, *, tt):
    m = p2.shape[0]
    a_blk = (V_END + D_DELTA) // D_CONV
    shift = _conformer_shift_matrix()
    return pl.pallas_call(
        functools.partial(_conformer_kernel, n_seg=tt // GRID_W),
        grid=(m // tt,),
        in_specs=[
            pl.BlockSpec((tt, D_CONV), lambda i: (i, a_blk)),
            pl.BlockSpec((tt, D_CONV), lambda i: (i, a_blk + 1)),
            pl.BlockSpec((CONF_W, D_CONV), lambda i: (0, 0)),
            pl.BlockSpec((1, D_CONV), lambda i: (0, 0)),
            pl.BlockSpec((1, D_CONV), lambda i: (0, 0)),
            pl.BlockSpec(shift.shape, lambda i: (0, 0)),
        ],
        out_specs=pl.BlockSpec((tt, D_CONV), lambda i: (i, 0)),
        out_shape=jax.ShapeDtypeStruct((m, D_CONV), BF16),
        scratch_shapes=[
            pltpu.VMEM((CONF_UNROLL, GRID_W + 2 * CONF_PAD, D_CONV), F32),
            pltpu.VMEM((CONF_UNROLL, (SUBLANES - 1) * CONF_SHIFT_ROWS, D_CONV), F32),
            pltpu.VMEM((GRID_W, D_CONV), F32),
        ],
        compiler_params=pltpu.CompilerParams(
            dimension_semantics=("arbitrary",), vmem_limit_bytes=VMEM_LIMIT),
        name="conformer",
    )(p2, p2, dw, lnw, lnb, shift)


def _outproj_kernel(of_ref, ob_ref, z_ref, dnw_ref, mc_ref, w1_ref, w2_ref, x_ref, g_ref, nw_ref, o_ref, *,
                    tiles_per_batch):
    b = pl.program_id(0) // tiles_per_batch
    dnw = dnw_ref[...]
    heads = []
    for h in range(N_HEADS):
        lo = h * HEAD_DIM
        oh = of_ref[:, lo:lo + HEAD_DIM].astype(F32) + ob_ref[:, lo:lo + HEAD_DIM].astype(F32)
        heads.append((_rms(oh, dnw) * _silu(z_ref[:, lo:lo + HEAD_DIM])).astype(BF16))
    md = jnp.concatenate(heads, axis=1)
    y = (jnp.dot(md, w1_ref[...], preferred_element_type=F32)
         + jnp.dot(mc_ref[...], w2_ref[...], preferred_element_type=F32))
    g = g_ref[pl.ds(b, 1), :]
    o_ref[...] = x_ref[...] + g * _rms(y, nw_ref[...])


def _outproj_call(o_f, o_b, p2, dnw, mc, w_out, x2, mod, norm_w, *, tm, tiles_per_batch):
    m = x2.shape[0]
    z_blk = V_END // D_DELTA
    return pl.pallas_call(
        functools.partial(_outproj_kernel, tiles_per_batch=tiles_per_batch),
        grid=(m // tm,),
        in_specs=[
            pl.BlockSpec((tm, D_DELTA), lambda i: (i, 0)),
            pl.BlockSpec((tm, D_DELTA), lambda i: (i, 0)),
            pl.BlockSpec((tm, D_DELTA), lambda i: (i, z_blk)),
            pl.BlockSpec((1, HEAD_DIM), lambda i: (0, 0)),
            pl.BlockSpec((tm, D_CONV), lambda i: (i, 0)),
            pl.BlockSpec((D_DELTA, D_MODEL), lambda i: (0, 0)),
            pl.BlockSpec((D_CONV, D_MODEL), lambda i: (1, 0)),
            pl.BlockSpec((tm, D_MODEL), lambda i: (i, 0)),
            pl.BlockSpec((8, D_MODEL), lambda i: (0, 2)),
            pl.BlockSpec((1, D_MODEL), lambda i: (0, 0)),
        ],
        out_specs=pl.BlockSpec((tm, D_MODEL), lambda i: (i, 0)),
        out_shape=jax.ShapeDtypeStruct((m, D_MODEL), F32),
        compiler_params=pltpu.CompilerParams(
            dimension_semantics=("arbitrary",), vmem_limit_bytes=VMEM_LIMIT),
        name="out_proj",
    )(o_f, o_b, p2, dnw, mc, w_out, w_out, x2, mod, norm_w)


def _ffn_kernel(x_ref, nw_ref, sh_ref, sc_ref, g_ref, pw_ref, wg_ref, wu_ref, wd_ref, o_ref, h_ref, *,
                tiles_per_batch, n_k):
    i = pl.program_id(0)
    k = pl.program_id(1)
    b = i // tiles_per_batch

    @pl.when(k == 0)
    def _():
        sh = sh_ref[pl.ds(b, 1), :]
        sc = sc_ref[pl.ds(b, 1), :]
        h_ref[...] = (_rms(x_ref[...], nw_ref[...]) * (1.0 + sc) + sh).astype(BF16)
        o_ref[...] = jnp.zeros(o_ref.shape, F32)

    hh = h_ref[...]
    gate = jnp.dot(hh, wg_ref[...], preferred_element_type=F32)
    up = jnp.dot(hh, wu_ref[...], preferred_element_type=F32)
    act = (_silu(gate) * up).astype(BF16)
    o_ref[...] += jnp.dot(act, wd_ref[...], preferred_element_type=F32)

    @pl.when(k == n_k - 1)
    def _():
        g = g_ref[pl.ds(b, 1), :]
        o_ref[...] = x_ref[...] + g * _rms(o_ref[...], pw_ref[...])


def _ffn_call(x2, mod, norm_pre, norm_post, wg, wu, wd, *, tm, tf, tiles_per_batch):
    m = x2.shape[0]
    dff = wg.shape[1]
    n_k = dff // tf
    return pl.pallas_call(
        functools.partial(_ffn_kernel, tiles_per_batch=tiles_per_batch, n_k=n_k),
        grid=(m // tm, n_k),
        in_specs=[
            pl.BlockSpec((tm, D_MODEL), lambda i, k: (i, 0)),
            pl.BlockSpec((1, D_MODEL), lambda i, k: (0, 0)),
            pl.BlockSpec((8, D_MODEL), lambda i, k: (0, 3)),
            pl.BlockSpec((8, D_MODEL), lambda i, k: (0, 4)),
            pl.BlockSpec((8, D_MODEL), lambda i, k: (0, 5)),
            pl.BlockSpec((1, D_MODEL), lambda i, k: (0, 0)),
            pl.BlockSpec((D_MODEL, tf), lambda i, k: (0, k)),
            pl.BlockSpec((D_MODEL, tf), lambda i, k: (0, k)),
            pl.BlockSpec((tf, D_MODEL), lambda i, k: (k, 0)),
        ],
        out_specs=pl.BlockSpec((tm, D_MODEL), lambda i, k: (i, 0)),
        out_shape=jax.ShapeDtypeStruct((m, D_MODEL), F32),
        scratch_shapes=[pltpu.VMEM((tm, D_MODEL), BF16)],
        compiler_params=pltpu.CompilerParams(
            dimension_semantics=("arbitrary", "arbitrary"), vmem_limit_bytes=VMEM_LIMIT),
        name="ffn",
    )(x2, norm_pre, mod, mod, mod, norm_post, wg, wu, wd)


@jax.jit
def _forward(x, c, ctx, c_ctx, w_mod, b_mod, mix_norm_pre, mix_norm_post, w_in, qkv_conv,
             a_log, dt_bias, delta_out_norm, conf_dw, conf_ln_w, conf_ln_b, w_out,
             ffn_norm_pre, ffn_norm_post, w_gate, w_up, w_down):
    bsz, seq, d = x.shape
    ctx_len = ctx.shape[1]
    assert d == D_MODEL and w_mod.shape[0] == 1, "single-layer kernel"
    assert seq % 1024 == 0 and ctx_len % CHUNK == 0 and bsz + 1 <= 8

    cc = jnp.zeros((8, D_MODEL), F32).at[:bsz].set(c).at[bsz].set(c_ctx)
    mod = _mod_call(cc, w_mod[0], b_mod[0][None, :])

    w_main, w_g = _wprep_call(w_in[0])
    prm = (jnp.zeros((8, LANES), F32)
           .at[0, 2 * N_HEADS:N_FEAT].set(a_log[0].reshape(-1))
           .at[1, 2 * N_HEADS:N_FEAT].set(dt_bias[0].reshape(-1)))

    x2 = x.reshape(bsz * seq, D_MODEL)
    ctx2 = ctx.reshape(bsz * ctx_len, D_MODEL)
    tm = 1024
    p, ba = _inproj_call(x2, mix_norm_pre, mod, w_main, w_g, n_cols=D_MAIN, tm=tm, tn=1536,
                         tiles_per_mod_row=seq // tm, mod_row0=0)
    pc, bac = _inproj_call(ctx2, mix_norm_pre, mod, w_main, w_g, n_cols=V_END, tm=bsz * ctx_len, tn=1536,
                           tiles_per_mod_row=1, mod_row0=bsz)

    fc, fr = _gatefeat_call(ba.reshape(bsz, seq, LANES), prm)
    fc_c, fr_c = _gatefeat_call(bac.reshape(bsz, ctx_len, LANES), prm)
    qkv = _qkvconv_call(p.reshape(bsz, seq, D_MAIN), qkv_conv[0], tile=256, heads_per_step=1)
    qkv_c = _qkvconv_call(pc.reshape(bsz, ctx_len, V_END), qkv_conv[0], tile=256, heads_per_step=N_HEADS)

    zero_state = jnp.zeros((bsz, 2, N_HEADS, HEAD_DIM, HEAD_DIM), F32)
    _, _, s_ctx = _delta_call(qkv_c, fc_c, fr_c, zero_state, lb=ctx_len)
    o_f, o_b, _ = _delta_call(qkv, fc, fr, s_ctx, lb=512)

    mix_c = _conformer_call(p, conf_dw[0], conf_ln_w, conf_ln_b, tt=512)

    x1 = _outproj_call(o_f.reshape(bsz * seq, D_DELTA), o_b.reshape(bsz * seq, D_DELTA), p, delta_out_norm,
                       mix_c, w_out[0].astype(BF16), x2, mod, mix_norm_post,
                       tm=512, tiles_per_batch=seq // 512)
    out = _ffn_call(x1, mod, ffn_norm_pre, ffn_norm_post, w_gate[0].astype(BF16), w_up[0].astype(BF16),
                    w_down[0].astype(BF16), tm=512, tf=512, tiles_per_batch=seq // 512)
    return out.reshape(bsz, seq, D_MODEL)


def kernel(x, c, ctx, c_ctx, w_mod, b_mod, mix_norm_pre, mix_norm_post, w_in, qkv_conv, a_log, dt_bias,
           delta_out_norm, conf_dw, conf_ln_w, conf_ln_b, w_out, ffn_norm_pre, ffn_norm_post,
           w_gate, w_up, w_down):
    return _forward(x, c, ctx, c_ctx, w_mod, b_mod, mix_norm_pre, mix_norm_post, w_in, qkv_conv,
                    a_log, dt_bias, delta_out_norm, conf_dw, conf_ln_w, conf_ln_b, w_out,
                    ffn_norm_pre, ffn_norm_post, w_gate, w_up, w_down)
```

```python
import functools
import math

import jax
import jax.numpy as jnp
from jax import lax
from jax.experimental import pallas as pl
from jax.experimental.pallas import tpu as pltpu

F32 = jnp.float32
BF16 = jnp.bfloat16
HIGHEST = lax.Precision.HIGHEST

D_MODEL = 2048
N_HEADS = 8
HEAD_DIM = 128
D_DELTA = N_HEADS * HEAD_DIM
D_CONV = D_MODEL - D_DELTA
GRID_W = 64
CHUNK = 64
SHORT_W = 5
CONF_W = 31
EPS = 1e-6
V_END = 3 * D_DELTA
STATE_END = V_END + 4 * N_HEADS
Z_END = STATE_END + D_DELTA
D_MAIN = V_END + D_DELTA + 2 * D_CONV
N_FEAT = 4 * N_HEADS
LANES = 128
VMEM_LIMIT = 56 * 1024 * 1024

NT_DIMS = (((1,), (1,)), ((), ()))
TN_DIMS = (((0,), (0,)), ((), ()))


def _mm(a, b, dims=None):
    a = a.astype(BF16)
    b = b.astype(BF16)
    if dims is None:
        return jnp.dot(a, b, preferred_element_type=F32)
    return lax.dot_general(a, b, dims, preferred_element_type=F32)


def _silu(x):
    return x * jax.nn.sigmoid(x)


def _rms(x, w):
    return x * lax.rsqrt(jnp.mean(x * x, axis=-1, keepdims=True) + EPS) * w


def _mod_kernel(c_ref, w_ref, b_ref, o_ref):
    s = _silu(c_ref[...])
    s1 = s.astype(BF16)
    s2 = (s - s1.astype(F32)).astype(BF16)
    w = w_ref[...]
    w_hi = w.astype(BF16)
    w_lo = (w - w_hi.astype(F32)).astype(BF16)
    rows = s.shape[0]
    p_hi = jnp.dot(jnp.concatenate([s1, s2], axis=0), w_hi, preferred_element_type=F32)
    p_lo = jnp.dot(s1, w_lo, preferred_element_type=F32)
    o_ref[...] = p_hi[:rows] + p_hi[rows:] + p_lo + b_ref[...]


def _mod_call(cc, w_mod, b_mod):
    n = w_mod.shape[1]
    tn = 1024
    return pl.pallas_call(
        _mod_kernel,
        grid=(n // tn,),
        in_specs=[
            pl.BlockSpec((8, D_MODEL), lambda j: (0, 0)),
            pl.BlockSpec((D_MODEL, tn), lambda j: (0, j)),
            pl.BlockSpec((1, tn), lambda j: (0, j)),
        ],
        out_specs=pl.BlockSpec((8, tn), lambda j: (0, j)),
        out_shape=jax.ShapeDtypeStruct((8, n), F32),
        compiler_params=pltpu.CompilerParams(
            dimension_semantics=("arbitrary",), vmem_limit_bytes=VMEM_LIMIT),
        name="mod",
    )(cc, w_mod, b_mod)


def _inproj_kernel(x_ref, nw_ref, sh_ref, sc_ref, w_ref, wg_ref, o_ref, og_ref, h_ref, *,
                   tiles_per_mod_row, mod_row0):
    i = pl.program_id(0)
    j = pl.program_id(1)

    @pl.when(j == 0)
    def _():
        r = mod_row0 + i // tiles_per_mod_row
        sh = sh_ref[pl.ds(r, 1), :]
        sc = sc_ref[pl.ds(r, 1), :]
        hh = (_rms(x_ref[...], nw_ref[...]) * (1.0 + sc) + sh).astype(BF16)
        h_ref[...] = hh
        og_ref[...] = jnp.dot(hh, wg_ref[...], preferred_element_type=F32)

    o_ref[...] = jnp.dot(h_ref[...], w_ref[...], preferred_element_type=F32).astype(o_ref.dtype)


def _inproj_call(x2, norm_w, mod, w_main, w_g, *, n_cols, tm, tn, tiles_per_mod_row, mod_row0):
    m = x2.shape[0]
    kern = functools.partial(_inproj_kernel, tiles_per_mod_row=tiles_per_mod_row, mod_row0=mod_row0)
    return pl.pallas_call(
        kern,
        grid=(m // tm, n_cols // tn),
        in_specs=[
            pl.BlockSpec((tm, D_MODEL), lambda i, j: (i, 0)),
            pl.BlockSpec((1, D_MODEL), lambda i, j: (0, 0)),
            pl.BlockSpec((8, D_MODEL), lambda i, j: (0, 0)),
            pl.BlockSpec((8, D_MODEL), lambda i, j: (0, 1)),
            pl.BlockSpec((D_MODEL, tn), lambda i, j: (0, j)),
            pl.BlockSpec((D_MODEL, LANES), lambda i, j: (0, 0)),
        ],
        out_specs=[
            pl.BlockSpec((tm, tn), lambda i, j: (i, j)),
            pl.BlockSpec((tm, LANES), lambda i, j: (i, 0)),
        ],
        out_shape=[
            jax.ShapeDtypeStruct((m, n_cols), BF16),
            jax.ShapeDtypeStruct((m, LANES), F32),
        ],
        scratch_shapes=[pltpu.VMEM((tm, D_MODEL), BF16)],
        compiler_params=pltpu.CompilerParams(
            dimension_semantics=("arbitrary", "arbitrary"), vmem_limit_bytes=VMEM_LIMIT),
        name="in_proj",
    )(x2, norm_w, mod, mod, w_main, w_g)


GATE_UNROLL_MAX = 8


def _gatefeat_kernel(ba_ref, prm_ref, fc_ref, fr_ref, *, n_chunks, unroll):
    lane = lax.broadcasted_iota(jnp.int32, (CHUNK, LANES), 1)
    rr = lax.broadcasted_iota(jnp.int32, (CHUNK, CHUNK), 0)
    cc = lax.broadcasted_iota(jnp.int32, (CHUNK, CHUNK), 1)
    ltri = (rr >= cc).astype(F32)
    utri = (rr <= cc).astype(F32)
    r2 = lax.broadcasted_iota(jnp.int32, (LANES, LANES), 0)
    c2 = lax.broadcasted_iota(jnp.int32, (LANES, LANES), 1)
    eye = (r2 == c2).astype(F32)
    neg_a = -jnp.exp(prm_ref[0:1, :])
    dtb = prm_ref[1:2, :]

    def body(it, carry):
        idx = [it * unroll + u for u in range(unroll)]
        starts = [pl.multiple_of(ci * CHUNK, CHUNK) for ci in idx]
        xs = [ba_ref[pl.ds(s0, CHUNK), :] for s0 in starts]
        gs = []
        for x in xs:
            y = x + dtb
            gs.append(neg_a * (jnp.maximum(y, 0.0) + jnp.log1p(jnp.exp(-jnp.abs(y)))))
        pres = [jnp.dot(ltri, g, preferred_element_type=F32, precision=HIGHEST) for g in gs]
        sufs = [jnp.dot(utri, g, preferred_element_type=F32, precision=HIGHEST) for g in gs]
        fs = [jnp.where(lane < 2 * N_HEADS, jax.nn.sigmoid(x), jnp.where(lane < 3 * N_HEADS, pre, suf))
              for x, pre, suf in zip(xs, pres, sufs)]
        fts = [lax.dot_general(eye, f, NT_DIMS, preferred_element_type=F32, precision=HIGHEST) for f in fs]
        for ci, s0, f, ft in zip(idx, starts, fs, fts):
            fc_ref[pl.ds(s0, CHUNK), :] = f
            fr_ref[ci] = ft[0:N_FEAT, :]
        return carry

    lax.fori_loop(0, n_chunks // unroll, body, 0)


def _gatefeat_call(ba, prm):
    b, l, _ = ba.shape
    n_chunks = l // CHUNK
    unroll = math.gcd(n_chunks, GATE_UNROLL_MAX)
    return pl.pallas_call(
        functools.partial(_gatefeat_kernel, n_chunks=n_chunks, unroll=unroll),
        grid=(b,),
        in_specs=[
            pl.BlockSpec((None, l, LANES), lambda i: (i, 0, 0)),
            pl.BlockSpec((8, LANES), lambda i: (0, 0)),
        ],
        out_specs=[
            pl.BlockSpec((None, l, LANES), lambda i: (i, 0, 0)),
            pl.BlockSpec((None, n_chunks, N_FEAT, CHUNK), lambda i: (i, 0, 0, 0)),
        ],
        out_shape=[
            jax.ShapeDtypeStruct((b, l, LANES), F32),
            jax.ShapeDtypeStruct((b, n_chunks, N_FEAT, CHUNK), F32),
        ],
        compiler_params=pltpu.CompilerParams(
            dimension_semantics=("arbitrary",), vmem_limit_bytes=VMEM_LIMIT),
        name="gate_feat",
    )(ba, prm)


QKV_PAD = 8


def _qkvconv_kernel(x_ref, w_ref, o_ref, pad_ref, *, seq, tile, heads_per_step):
    j = pl.program_id(1)
    width = heads_per_step * HEAD_DIM
    zeros = jnp.zeros((QKV_PAD, width), F32)
    pad_ref[0:QKV_PAD, :] = zeros
    pad_ref[QKV_PAD + seq:2 * QKV_PAD + seq, :] = zeros
    pad_ref[QKV_PAD:QKV_PAD + seq, :] = x_ref[...].astype(F32)
    w = w_ref[...]
    for s in range(heads_per_step):
        cols = slice(s * HEAD_DIM, (s + 1) * HEAD_DIM)
        is_qk = j * heads_per_step + s < 2 * N_HEADS
        for t0 in range(0, seq, tile):
            acc = None
            for tap in range(SHORT_W):
                start = t0 + QKV_PAD - SHORT_W // 2 + tap
                term = pad_ref[start:start + tile, cols] * w[tap:tap + 1, cols]
                acc = term if acc is None else acc + term
            y = _silu(acc)
            inv = lax.rsqrt(jnp.sum(y * y, axis=-1, keepdims=True) + EPS)
            o_ref[t0:t0 + tile, cols] = (y * jnp.where(is_qk, inv, 1.0)).astype(o_ref.dtype)


def _qkvconv_call(p, conv_w, *, tile, heads_per_step):
    b, l, _ = p.shape
    width = heads_per_step * HEAD_DIM
    n_blocks = V_END // width
    return pl.pallas_call(
        functools.partial(_qkvconv_kernel, seq=l, tile=tile, heads_per_step=heads_per_step),
        grid=(b, n_blocks),
        in_specs=[
            pl.BlockSpec((None, l, width), lambda i, j: (i, 0, j)),
            pl.BlockSpec((SHORT_W, width), lambda i, j: (0, j)),
        ],
        out_specs=pl.BlockSpec((None, l, width), lambda i, j: (i, 0, j)),
        out_shape=jax.ShapeDtypeStruct((b, l, V_END), BF16),
        scratch_shapes=[pltpu.VMEM((l + 2 * QKV_PAD, width), F32)],
        compiler_params=pltpu.CompilerParams(
            dimension_semantics=("arbitrary", "arbitrary"), vmem_limit_bytes=VMEM_LIMIT),
        name="qkv_conv",
    )(p, conv_w)


def _delta_kernel(xf_ref, xb_ref, fcf_ref, fcb_ref, frf_ref, frb_ref, init_ref,
                  of_ref, ob_ref, st_ref, *, n_pos):
    @pl.when(pl.program_id(1) == 0)
    def _():
        st_ref[...] = init_ref[...]

    rr = lax.broadcasted_iota(jnp.int32, (CHUNK, CHUNK), 0)
    cc = lax.broadcasted_iota(jnp.int32, (CHUNK, CHUNK), 1)
    eye = (rr == cc).astype(F32)
    q_scale = HEAD_DIM ** -0.5

    class Chain:
        def __init__(self, x_ref, fc_ref, fr_ref, o_ref, c, h, forward):
            self.x_ref, self.fc_ref, self.fr_ref, self.o_ref = x_ref, fc_ref, fr_ref, o_ref
            self.c, self.h, self.forward = c, h, forward
            self.d = 0 if forward else 1
            self.s0 = pl.multiple_of(c * CHUNK, CHUNK)
            self.lo = h * HEAD_DIM

        def tile(self, which):
            off = which * D_DELTA + self.lo
            return self.x_ref[pl.ds(self.s0, CHUNK), off:off + HEAD_DIM].astype(F32)

        def cols(self):
            f = self.fc_ref[pl.ds(self.s0, CHUNK), :]
            ib = self.d * N_HEADS + self.h
            ig = (2 + self.d) * N_HEADS + self.h
            shape = (CHUNK, HEAD_DIM)
            return jnp.broadcast_to(f[:, ib:ib + 1], shape), jnp.broadcast_to(f[:, ig:ig + 1], shape)

        def g_row(self):
            ig = (2 + self.d) * N_HEADS + self.h
            return self.fr_ref[self.c, ig:ig + 1, :]

    def stage_gram(ch):
        ch.beta_b, ch.gc_b = ch.cols()
        k = ch.tile(1)
        ch.gram = _mm(jnp.concatenate([ch.tile(0) * q_scale, k * ch.beta_b], axis=0), k, NT_DIMS)
        incl = (rr >= cc) if ch.forward else (rr <= cc)
        ch.decay = jnp.where(incl, jnp.exp(jnp.where(incl, ch.gc_b[:, :CHUNK] - ch.g_row(), 0.0)), 0.0)

    def stage_power0(ch):
        strict = (rr > cc) if ch.forward else (rr < cc)
        ch.attn = (ch.gram[:CHUNK] * ch.decay).astype(BF16)
        neg = -jnp.where(strict, ch.gram[CHUNK:] * ch.decay, 0.0)
        ch.neg = neg.astype(BF16)
        ch.t_inv = eye + neg
        ch.pw = _mm(ch.neg, ch.neg)
        del ch.gram, ch.decay

    def stage_power(ch):
        both = _mm(jnp.concatenate([ch.pw, ch.t_inv], axis=0), ch.pw)
        ch.t_inv = ch.t_inv + both[CHUNK:]
        ch.pw = both[:CHUNK]

    def stage_power_last(ch):
        ch.t_inv = (ch.t_inv + _mm(ch.t_inv, ch.pw)).astype(BF16)
        del ch.pw

    def stage_newton_residual(ch):
        ch.err = eye - ch.t_inv.astype(F32) + _mm(ch.neg, ch.t_inv)
        del ch.neg

    def stage_newton_apply(ch):
        ch.t_inv = (ch.t_inv.astype(F32) + _mm(ch.t_inv, ch.err)).astype(BF16)
        del ch.err

    def stage_predict(ch):
        eg_b = jnp.exp(ch.gc_b)
        kbg = ch.tile(1) * (ch.beta_b * eg_b)
        qd = ch.tile(0) * (q_scale * eg_b)
        ps = _mm(jnp.concatenate([kbg, qd], axis=0), st_ref[ch.d, ch.h])
        ch.resid = ch.tile(2) * ch.beta_b - ps[:CHUNK]
        ch.qs = ps[CHUNK:]
        del ch.beta_b

    def stage_solve(ch):
        ch.v_new = _mm(ch.t_inv, ch.resid).astype(BF16)
        del ch.resid, ch.t_inv

    def stage_update(ch):
        g_row = ch.g_row()
        g_end = g_row[:, CHUNK - 1:CHUNK] if ch.forward else g_row[:, 0:1]
        ch.o_ref[pl.ds(ch.s0, CHUNK), ch.lo:ch.lo + HEAD_DIM] = (
            ch.qs + _mm(ch.attn, ch.v_new)).astype(ch.o_ref.dtype)
        k_dec = ch.tile(1) * jnp.exp(g_end - ch.gc_b)
        st_ref[ch.d, ch.h] = st_ref[ch.d, ch.h] * jnp.exp(g_end) + _mm(k_dec, ch.v_new, TN_DIMS)

    stages = ([stage_gram, stage_power0] + [stage_power] * 4
              + [stage_power_last, stage_newton_residual, stage_newton_apply,
                 stage_predict, stage_solve, stage_update])

    def body(c, carry):
        cb = n_pos - 1 - c
        chains = []
        for h in range(N_HEADS):
            chains.append(Chain(xf_ref, fcf_ref, frf_ref, of_ref, c, h, True))
            chains.append(Chain(xb_ref, fcb_ref, frb_ref, ob_ref, cb, h, False))
        for stage in stages:
            for ch in chains:
                stage(ch)
        return carry

    lax.fori_loop(0, n_pos, body, 0)


def _delta_call(qkv, fc, fr, init, *, lb):
    b, l, _ = qkv.shape
    nb = l // lb
    n_pos = lb // CHUNK
    fwd3 = lambda i, j: (i, j, 0)
    bwd3 = lambda i, j: (i, nb - 1 - j, 0)
    st_spec = pl.BlockSpec((None, 2, N_HEADS, HEAD_DIM, HEAD_DIM), lambda i, j: (i, 0, 0, 0, 0))
    return pl.pallas_call(
        functools.partial(_delta_kernel, n_pos=n_pos),
        grid=(b, nb),
        in_specs=[
            pl.BlockSpec((None, lb, V_END), fwd3),
            pl.BlockSpec((None, lb, V_END), bwd3),
            pl.BlockSpec((None, lb, LANES), fwd3),
            pl.BlockSpec((None, lb, LANES), bwd3),
            pl.BlockSpec((None, n_pos, N_FEAT, CHUNK), lambda i, j: (i, j, 0, 0)),
            pl.BlockSpec((None, n_pos, N_FEAT, CHUNK), lambda i, j: (i, nb - 1 - j, 0, 0)),
            st_spec,
        ],
        out_specs=[
            pl.BlockSpec((None, lb, D_DELTA), fwd3),
            pl.BlockSpec((None, lb, D_DELTA), bwd3),
            st_spec,
        ],
        out_shape=[
            jax.ShapeDtypeStruct((b, l, D_DELTA), BF16),
            jax.ShapeDtypeStruct((b, l, D_DELTA), BF16),
            jax.ShapeDtypeStruct((b, 2, N_HEADS, HEAD_DIM, HEAD_DIM), F32),
        ],
        compiler_params=pltpu.CompilerParams(
            dimension_semantics=("arbitrary", "arbitrary"), vmem_limit_bytes=VMEM_LIMIT),
        name="delta",
    )(qkv, qkv, fc, fc, fr, fr, init)


CONF_PAD = 16
CONF_CCHUNK = 256
SUBLANES = 8
CONF_SHIFT_ROWS = GRID_W + 2 * CONF_PAD - SUBLANES
CONF_UNROLL = 4


def _conformer_kernel(a_ref, g_ref, dw_ref, lnw_ref, lnb_ref, shift_ref, o_ref, pad_ref, sh_ref, cv_ref, *,
                      n_seg):
    zeros = jnp.zeros((CONF_PAD, D_CONV), F32)
    for slot in range(CONF_UNROLL):
        pad_ref[slot, 0:CONF_PAD, :] = zeros
        pad_ref[slot, CONF_PAD + GRID_W:2 * CONF_PAD + GRID_W, :] = zeros
    lnw = lnw_ref[...]
    lnb = lnb_ref[...]

    def realign(slot, r0):
        pad_ref[slot, CONF_PAD:CONF_PAD + GRID_W, :] = (
            a_ref[pl.ds(r0, GRID_W), :].astype(F32)
            * jax.nn.sigmoid(g_ref[pl.ds(r0, GRID_W), :].astype(F32)))
        padded = pad_ref[slot]
        hi = padded.astype(BF16)
        lo = (padded - hi.astype(F32)).astype(BF16)
        sh_ref[slot] = jnp.dot(shift_ref[...], jnp.concatenate([hi, lo], axis=0),
                               preferred_element_type=F32)

    def taps(slot, r0):
        for c0 in range(0, D_CONV, CONF_CCHUNK):
            cols = slice(c0, c0 + CONF_CCHUNK)
            acc = None
            for tap in range(CONF_W):
                start = CONF_PAD - CONF_W // 2 + tap
                r = start % SUBLANES
                base = start - r
                if r == 0:
                    win = pad_ref[slot, base:base + GRID_W, cols]
                else:
                    row = (r - 1) * CONF_SHIFT_ROWS + base
                    win = sh_ref[slot, row:row + GRID_W, cols]
                term = win * dw_ref[tap:tap + 1, cols]
                acc = term if acc is None else acc + term
            cv_ref[:, cols] = acc
        u = cv_ref[...]
        mu = jnp.mean(u, axis=-1, keepdims=True)
        uc = u - mu
        var = jnp.mean(uc * uc, axis=-1, keepdims=True)
        y = uc * lax.rsqrt(var + EPS) * lnw + lnb
        o_ref[pl.ds(r0, GRID_W), :] = _silu(y).astype(o_ref.dtype)

    def body(it, carry):
        starts = [pl.multiple_of((it * CONF_UNROLL + u) * GRID_W, GRID_W) for u in range(CONF_UNROLL)]
        for slot, r0 in enumerate(starts):
            realign(slot, r0)
        for slot, r0 in enumerate(starts):
            taps(slot, r0)
        return carry

    lax.fori_loop(0, n_seg // CONF_UNROLL, body, 0)


def _conformer_shift_matrix():
    padded_rows = GRID_W + 2 * CONF_PAD
    r = jnp.arange(1, SUBLANES)[:, None, None]
    i = jnp.arange(CONF_SHIFT_ROWS)[None, :, None]
    col = jnp.arange(2 * padded_rows)[None, None, :]
    hit = (col == i + r) | (col == padded_rows + i + r)
    return hit.reshape((SUBLANES - 1) * CONF_SHIFT_ROWS, 2 * padded_rows).astype(BF16)


def _conformer_call(p2, dw, lnw, lnb, *, tt):
    m = p2.shape[0]
    a_blk = (V_END + D_DELTA) // D_CONV
    shift = _conformer_shift_matrix()
    return pl.pallas_call(
        functools.partial(_conformer_kernel, n_seg=tt // GRID_W),
        grid=(m // tt,),
        in_specs=[
            pl.BlockSpec((tt, D_CONV), lambda i: (i, a_blk)),
            pl.BlockSpec((tt, D_CONV), lambda i: (i, a_blk + 1)),
            pl.BlockSpec((CONF_W, D_CONV), lambda i: (0, 0)),
            pl.BlockSpec((1, D_CONV), lambda i: (0, 0)),
            pl.BlockSpec((1, D_CONV), lambda i: (0, 0)),
            pl.BlockSpec(shift.shape, lambda i: (0, 0)),
        ],
        out_specs=pl.BlockSpec((tt, D_CONV), lambda i: (i, 0)),
        out_shape=jax.ShapeDtypeStruct((m, D_CONV), BF16),
        scratch_shapes=[
            pltpu.VMEM((CONF_UNROLL, GRID_W + 2 * CONF_PAD, D_CONV), F32),
            pltpu.VMEM((CONF_UNROLL, (SUBLANES - 1) * CONF_SHIFT_ROWS, D_CONV), F32),
            pltpu.VMEM((GRID_W, D_CONV), F32),
        ],
        compiler_params=pltpu.CompilerParams(
            dimension_semantics=("arbitrary",), vmem_limit_bytes=VMEM_LIMIT),
        name="conformer",
    )(p2, p2, dw, lnw, lnb, shift)


def _outproj_kernel(of_ref, ob_ref, z_ref, dnw_ref, mc_ref, w1_ref, w2_ref, x_ref, g_ref, nw_ref, o_ref, *,
                    tiles_per_batch):
    b = pl.program_id(0) // tiles_per_batch
    dnw = dnw_ref[...]
    heads = []
    for h in range(N_HEADS):
        lo = h * HEAD_DIM
        oh = of_ref[:, lo:lo + HEAD_DIM].astype(F32) + ob_ref[:, lo:lo + HEAD_DIM].astype(F32)
        heads.append((_rms(oh, dnw) * _silu(z_ref[:, lo:lo + HEAD_DIM].astype(F32))).astype(BF16))
    md = jnp.concatenate(heads, axis=1)
    y = (jnp.dot(md, w1_ref[...], preferred_element_type=F32)
         + jnp.dot(mc_ref[...], w2_ref[...], preferred_element_type=F32))
    g = g_ref[pl.ds(b, 1), :]
    o_ref[...] = x_ref[...] + g * _rms(y, nw_ref[...])


def _outproj_call(o_f, o_b, p2, dnw, mc, w_out, x2, mod, norm_w, *, tm, tiles_per_batch):
    m = x2.shape[0]
    z_blk = V_END // D_DELTA
    return pl.pallas_call(
        functools.partial(_outproj_kernel, tiles_per_batch=tiles_per_batch),
        grid=(m // tm,),
        in_specs=[
            pl.BlockSpec((tm, D_DELTA), lambda i: (i, 0)),
            pl.BlockSpec((tm, D_DELTA), lambda i: (i, 0)),
            pl.BlockSpec((tm, D_DELTA), lambda i: (i, z_blk)),
            pl.BlockSpec((1, HEAD_DIM), lambda i: (0, 0)),
            pl.BlockSpec((tm, D_CONV), lambda i: (i, 0)),
            pl.BlockSpec((D_DELTA, D_MODEL), lambda i: (0, 0)),
            pl.BlockSpec((D_CONV, D_MODEL), lambda i: (1, 0)),
            pl.BlockSpec((tm, D_MODEL), lambda i: (i, 0)),
            pl.BlockSpec((8, D_MODEL), lambda i: (0, 2)),
            pl.BlockSpec((1, D_MODEL), lambda i: (0, 0)),
        ],
        out_specs=pl.BlockSpec((tm, D_MODEL), lambda i: (i, 0)),
        out_shape=jax.ShapeDtypeStruct((m, D_MODEL), F32),
        compiler_params=pltpu.CompilerParams(
            dimension_semantics=("arbitrary",), vmem_limit_bytes=VMEM_LIMIT),
        name="out_proj",
    )(o_f, o_b, p2, dnw, mc, w_out, w_out, x2, mod, norm_w)


def _ffn_kernel(x_ref, nw_ref, sh_ref, sc_ref, g_ref, pw_ref, wg_ref, wu_ref, wd_ref, o_ref, h_ref, *,
                tiles_per_batch, n_k):
    i = pl.program_id(0)
    k = pl.program_id(1)
    b = i // tiles_per_batch

    @pl.when(k == 0)
    def _():
        sh = sh_ref[pl.ds(b, 1), :]
        sc = sc_ref[pl.ds(b, 1), :]
        h_ref[...] = (_rms(x_ref[...], nw_ref[...]) * (1.0 + sc) + sh).astype(BF16)
        o_ref[...] = jnp.zeros(o_ref.shape, F32)

    hh = h_ref[...]
    gate = jnp.dot(hh, wg_ref[...], preferred_element_type=F32)
    up = jnp.dot(hh, wu_ref[...], preferred_element_type=F32)
    act = (_silu(gate) * up).astype(BF16)
    o_ref[...] += jnp.dot(act, wd_ref[...], preferred_element_type=F32)

    @pl.when(k == n_k - 1)
    def _():
        g = g_ref[pl.ds(b, 1), :]
        o_ref[...] = x_ref[...] + g * _rms(o_ref[...], pw_ref[...])


def _ffn_call(x2, mod, norm_pre, norm_post, wg, wu, wd, *, tm, tf, tiles_per_batch):
    m = x2.shape[0]
    dff = wg.shape[1]
    n_k = dff // tf
    return pl.pallas_call(
        functools.partial(_ffn_kernel, tiles_per_batch=tiles_per_batch, n_k=n_k),
        grid=(m // tm, n_k),
        in_specs=[
            pl.BlockSpec((tm, D_MODEL), lambda i, k: (i, 0)),
            pl.BlockSpec((1, D_MODEL), lambda i, k: (0, 0)),
            pl.BlockSpec((8, D_MODEL), lambda i, k: (0, 3)),
            pl.BlockSpec((8, D_MODEL), lambda i, k: (0, 4)),
            pl.BlockSpec((8, D_MODEL), lambda i, k: (0, 5)),
            pl.BlockSpec((1, D_MODEL), lambda i, k: (0, 0)),
            pl.BlockSpec((D_MODEL, tf), lambda i, k: (0, k)),
            pl.BlockSpec((D_MODEL, tf), lambda i, k: (0, k)),
            pl.BlockSpec((tf, D_MODEL), lambda i, k: (k, 0)),
        ],
        out_specs=pl.BlockSpec((tm, D_MODEL), lambda i, k: (i, 0)),
        out_shape=jax.ShapeDtypeStruct((m, D_MODEL), F32),
        scratch_shapes=[pltpu.VMEM((tm, D_MODEL), BF16)],
        compiler_params=pltpu.CompilerParams(
            dimension_semantics=("arbitrary", "arbitrary"), vmem_limit_bytes=VMEM_LIMIT),
        name="ffn",
    )(x2, norm_pre, mod, mod, mod, norm_post, wg, wu, wd)


@jax.jit
def _forward(x, c, ctx, c_ctx, w_mod, b_mod, mix_norm_pre, mix_norm_post, w_in, qkv_conv,
             a_log, dt_bias, delta_out_norm, conf_dw, conf_ln_w, conf_ln_b, w_out,
             ffn_norm_pre, ffn_norm_post, w_gate, w_up, w_down):
    bsz, seq, d = x.shape
    ctx_len = ctx.shape[1]
    assert d == D_MODEL and w_mod.shape[0] == 1, "single-layer kernel"
    assert seq % 1024 == 0 and ctx_len % CHUNK == 0 and bsz + 1 <= 8

    cc = jnp.zeros((8, D_MODEL), F32).at[:bsz].set(c).at[bsz].set(c_ctx)
    mod = _mod_call(cc, w_mod[0], b_mod[0][None, :])

    w_in0 = w_in[0]
    w_main = jnp.concatenate([w_in0[:, :V_END].astype(BF16), w_in0[:, STATE_END:].astype(BF16)], axis=1)
    w_g = jnp.zeros((D_MODEL, LANES), BF16).at[:, :N_FEAT].set(w_in0[:, V_END:STATE_END].astype(BF16))
    prm = (jnp.zeros((8, LANES), F32)
           .at[0, 2 * N_HEADS:N_FEAT].set(a_log[0].reshape(-1))
           .at[1, 2 * N_HEADS:N_FEAT].set(dt_bias[0].reshape(-1)))

    x2 = x.reshape(bsz * seq, D_MODEL)
    ctx2 = ctx.reshape(bsz * ctx_len, D_MODEL)
    tm = 1024
    p, ba = _inproj_call(x2, mix_norm_pre, mod, w_main, w_g, n_cols=D_MAIN, tm=tm, tn=1536,
                         tiles_per_mod_row=seq // tm, mod_row0=0)
    pc, bac = _inproj_call(ctx2, mix_norm_pre, mod, w_main, w_g, n_cols=V_END, tm=bsz * ctx_len, tn=1536,
                           tiles_per_mod_row=1, mod_row0=bsz)

    fc, fr = _gatefeat_call(ba.reshape(bsz, seq, LANES), prm)
    fc_c, fr_c = _gatefeat_call(bac.reshape(bsz, ctx_len, LANES), prm)
    qkv = _qkvconv_call(p.reshape(bsz, seq, D_MAIN), qkv_conv[0], tile=256, heads_per_step=1)
    qkv_c = _qkvconv_call(pc.reshape(bsz, ctx_len, V_END), qkv_conv[0], tile=256, heads_per_step=N_HEADS)

    zero_state = jnp.zeros((bsz, 2, N_HEADS, HEAD_DIM, HEAD_DIM), F32)
    _, _, s_ctx = _delta_call(qkv_c, fc_c, fr_c, zero_state, lb=ctx_len)
    o_f, o_b, _ = _delta_call(qkv, fc, fr, s_ctx, lb=1024)

    mix_c = _conformer_call(p, conf_dw[0], conf_ln_w, conf_ln_b, tt=512)

    x1 = _outproj_call(o_f.reshape(bsz * seq, D_DELTA), o_b.reshape(bsz * seq, D_DELTA), p, delta_out_norm,
                       mix_c, w_out[0].astype(BF16), x2, mod, mix_norm_post,
                       tm=512, tiles_per_batch=seq // 512)
    out = _ffn_call(x1, mod, ffn_norm_pre, ffn_norm_post, w_gate[0].astype(BF16), w_up[0].astype(BF16),
                    w_down[0].astype(BF16), tm=512, tf=512, tiles_per_batch=seq // 512)
    return out.reshape(bsz, seq, D_MODEL)


def kernel(x, c, ctx, c_ctx, w_mod, b_mod, mix_norm_pre, mix_norm_post, w_in, qkv_conv, a_log, dt_bias,
           delta_out_norm, conf_dw, conf_ln_w, conf_ln_b, w_out, ffn_norm_pre, ffn_norm_post,
           w_gate, w_up, w_down):
    return _forward(x, c, ctx, c_ctx, w_mod, b_mod, mix_norm_pre, mix_norm_post, w_in, qkv_conv,
                    a_log, dt_bias, delta_out_norm, conf_dw, conf_ln_w, conf_ln_b, w_out,
                    ffn_norm_pre, ffn_norm_post, w_gate, w_up, w_down)
```

```python
import functools
import math
from typing import NamedTuple

import jax
import jax.numpy as jnp
from jax import lax
from jax.experimental import pallas as pl
from jax.experimental.pallas import tpu as pltpu

F32 = jnp.float32
BF16 = jnp.bfloat16
HIGHEST = lax.Precision.HIGHEST

D_MODEL = 2048
N_HEADS = 8
HEAD_DIM = 128
D_DELTA = N_HEADS * HEAD_DIM
D_CONV = D_MODEL - D_DELTA
GRID_W = 64
CHUNK = 64
SHORT_W = 5
CONF_W = 31
EPS = 1e-6
V_END = 3 * D_DELTA
STATE_END = V_END + 4 * N_HEADS
Z_END = STATE_END + D_DELTA
D_MAIN = V_END + D_DELTA + 2 * D_CONV
N_FEAT = 4 * N_HEADS
LANES = 128
VMEM_LIMIT = 56 * 1024 * 1024


class _Tiles(NamedTuple):
    inproj_tm: int = 1024
    inproj_tn: int = 1536
    qkv_tile: int = 256
    delta_block: int = 512
    conf_rows: int = 512
    outproj_tm: int = 512
    ffn_tm: int = 512
    ffn_tf: int = 512


_TILES = _Tiles()

NT_DIMS = (((1,), (1,)), ((), ()))
TN_DIMS = (((0,), (0,)), ((), ()))


def _mm(a, b, dims=None):
    a = a.astype(BF16)
    b = b.astype(BF16)
    if dims is None:
        return jnp.dot(a, b, preferred_element_type=F32)
    return lax.dot_general(a, b, dims, preferred_element_type=F32)


def _silu(x):
    return x * jax.nn.sigmoid(x)


def _rms(x, w):
    return x * lax.rsqrt(jnp.mean(x * x, axis=-1, keepdims=True) + EPS) * w


def _mod_kernel(c_ref, w_ref, b_ref, o_ref):
    s = _silu(c_ref[...])
    s1 = s.astype(BF16)
    s2 = (s - s1.astype(F32)).astype(BF16)
    w = w_ref[...]
    w_hi = w.astype(BF16)
    w_lo = (w - w_hi.astype(F32)).astype(BF16)
    rows = s.shape[0]
    p_hi = jnp.dot(jnp.concatenate([s1, s2], axis=0), w_hi, preferred_element_type=F32)
    p_lo = jnp.dot(s1, w_lo, preferred_element_type=F32)
    o_ref[...] = p_hi[:rows] + p_hi[rows:] + p_lo + b_ref[...]


def _mod_call(cc, w_mod, b_mod):
    n = w_mod.shape[1]
    tn = 1024
    return pl.pallas_call(
        _mod_kernel,
        grid=(n // tn,),
        in_specs=[
            pl.BlockSpec((8, D_MODEL), lambda j: (0, 0)),
            pl.BlockSpec((D_MODEL, tn), lambda j: (0, j)),
            pl.BlockSpec((1, tn), lambda j: (0, j)),
        ],
        out_specs=pl.BlockSpec((8, tn), lambda j: (0, j)),
        out_shape=jax.ShapeDtypeStruct((8, n), F32),
        compiler_params=pltpu.CompilerParams(
            dimension_semantics=("arbitrary",), vmem_limit_bytes=VMEM_LIMIT),
        name="mod",
    )(cc, w_mod, b_mod)


def _inproj_kernel(x_ref, nw_ref, sh_ref, sc_ref, w_ref, wg_ref, o_ref, og_ref, h_ref, *,
                   tiles_per_mod_row, mod_row0):
    i = pl.program_id(0)
    j = pl.program_id(1)

    @pl.when(j == 0)
    def _():
        r = mod_row0 + i // tiles_per_mod_row
        sh = sh_ref[pl.ds(r, 1), :]
        sc = sc_ref[pl.ds(r, 1), :]
        hh = (_rms(x_ref[...], nw_ref[...]) * (1.0 + sc) + sh).astype(BF16)
        h_ref[...] = hh
        og_ref[...] = jnp.dot(hh, wg_ref[...], preferred_element_type=F32)

    o_ref[...] = jnp.dot(h_ref[...], w_ref[...], preferred_element_type=F32)


def _inproj_call(x2, norm_w, mod, w_main, w_g, *, n_cols, tm, tn, tiles_per_mod_row, mod_row0):
    m = x2.shape[0]
    kern = functools.partial(_inproj_kernel, tiles_per_mod_row=tiles_per_mod_row, mod_row0=mod_row0)
    return pl.pallas_call(
        kern,
        grid=(m // tm, n_cols // tn),
        in_specs=[
            pl.BlockSpec((tm, D_MODEL), lambda i, j: (i, 0)),
            pl.BlockSpec((1, D_MODEL), lambda i, j: (0, 0)),
            pl.BlockSpec((8, D_MODEL), lambda i, j: (0, 0)),
            pl.BlockSpec((8, D_MODEL), lambda i, j: (0, 1)),
            pl.BlockSpec((D_MODEL, tn), lambda i, j: (0, j)),
            pl.BlockSpec((D_MODEL, LANES), lambda i, j: (0, 0)),
        ],
        out_specs=[
            pl.BlockSpec((tm, tn), lambda i, j: (i, j)),
            pl.BlockSpec((tm, LANES), lambda i, j: (i, 0)),
        ],
        out_shape=[
            jax.ShapeDtypeStruct((m, n_cols), F32),
            jax.ShapeDtypeStruct((m, LANES), F32),
        ],
        scratch_shapes=[pltpu.VMEM((tm, D_MODEL), BF16)],
        compiler_params=pltpu.CompilerParams(
            dimension_semantics=("arbitrary", "arbitrary"), vmem_limit_bytes=VMEM_LIMIT),
        name="in_proj",
    )(x2, norm_w, mod, mod, w_main, w_g)


GATE_UNROLL_MAX = 8


def _gatefeat_kernel(ba_ref, prm_ref, fc_ref, fr_ref, *, n_chunks, unroll):
    lane = lax.broadcasted_iota(jnp.int32, (CHUNK, LANES), 1)
    rr = lax.broadcasted_iota(jnp.int32, (CHUNK, CHUNK), 0)
    cc = lax.broadcasted_iota(jnp.int32, (CHUNK, CHUNK), 1)
    ltri = (rr >= cc).astype(F32)
    utri = (rr <= cc).astype(F32)
    r2 = lax.broadcasted_iota(jnp.int32, (LANES, LANES), 0)
    c2 = lax.broadcasted_iota(jnp.int32, (LANES, LANES), 1)
    eye = (r2 == c2).astype(F32)
    neg_a = -jnp.exp(prm_ref[0:1, :])
    dtb = prm_ref[1:2, :]

    def body(it, carry):
        idx = [it * unroll + u for u in range(unroll)]
        starts = [pl.multiple_of(ci * CHUNK, CHUNK) for ci in idx]
        xs = [ba_ref[pl.ds(s0, CHUNK), :] for s0 in starts]
        gs = []
        for x in xs:
            y = x + dtb
            gs.append(neg_a * (jnp.maximum(y, 0.0) + jnp.log1p(jnp.exp(-jnp.abs(y)))))
        pres = [jnp.dot(ltri, g, preferred_element_type=F32, precision=HIGHEST) for g in gs]
        sufs = [jnp.dot(utri, g, preferred_element_type=F32, precision=HIGHEST) for g in gs]
        fs = [jnp.where(lane < 2 * N_HEADS, jax.nn.sigmoid(x), jnp.where(lane < 3 * N_HEADS, pre, suf))
              for x, pre, suf in zip(xs, pres, sufs)]
        fts = [lax.dot_general(eye, f, NT_DIMS, preferred_element_type=F32, precision=HIGHEST) for f in fs]
        for ci, s0, f, ft in zip(idx, starts, fs, fts):
            fc_ref[pl.ds(s0, CHUNK), :] = f
            fr_ref[ci] = ft[0:N_FEAT, :]
        return carry

    lax.fori_loop(0, n_chunks // unroll, body, 0)


def _gatefeat_call(ba, prm):
    b, l, _ = ba.shape
    n_chunks = l // CHUNK
    unroll = math.gcd(n_chunks, GATE_UNROLL_MAX)
    return pl.pallas_call(
        functools.partial(_gatefeat_kernel, n_chunks=n_chunks, unroll=unroll),
        grid=(b,),
        in_specs=[
            pl.BlockSpec((None, l, LANES), lambda i: (i, 0, 0)),
            pl.BlockSpec((8, LANES), lambda i: (0, 0)),
        ],
        out_specs=[
            pl.BlockSpec((None, l, LANES), lambda i: (i, 0, 0)),
            pl.BlockSpec((None, n_chunks, N_FEAT, CHUNK), lambda i: (i, 0, 0, 0)),
        ],
        out_shape=[
            jax.ShapeDtypeStruct((b, l, LANES), F32),
            jax.ShapeDtypeStruct((b, n_chunks, N_FEAT, CHUNK), F32),
        ],
        compiler_params=pltpu.CompilerParams(
            dimension_semantics=("arbitrary",), vmem_limit_bytes=VMEM_LIMIT),
        name="gate_feat",
    )(ba, prm)


QKV_PAD = 8


def _qkvconv_kernel(x_ref, w_ref, o_ref, pad_ref, *, seq, tile, heads_per_step):
    j = pl.program_id(1)
    width = heads_per_step * HEAD_DIM
    zeros = jnp.zeros((QKV_PAD, width), F32)
    pad_ref[0:QKV_PAD, :] = zeros
    pad_ref[QKV_PAD + seq:2 * QKV_PAD + seq, :] = zeros
    pad_ref[QKV_PAD:QKV_PAD + seq, :] = x_ref[...]
    w = w_ref[...]
    for s in range(heads_per_step):
        cols = slice(s * HEAD_DIM, (s + 1) * HEAD_DIM)
        is_qk = j * heads_per_step + s < 2 * N_HEADS
        for t0 in range(0, seq, tile):
            acc = None
            for tap in range(SHORT_W):
                start = t0 + QKV_PAD - SHORT_W // 2 + tap
                term = pad_ref[start:start + tile, cols] * w[tap:tap + 1, cols]
                acc = term if acc is None else acc + term
            y = _silu(acc)
            inv = lax.rsqrt(jnp.sum(y * y, axis=-1, keepdims=True) + EPS)
            o_ref[t0:t0 + tile, cols] = (y * jnp.where(is_qk, inv, 1.0)).astype(o_ref.dtype)


def _qkvconv_call(p, conv_w, *, tile, heads_per_step):
    b, l, _ = p.shape
    width = heads_per_step * HEAD_DIM
    n_blocks = V_END // width
    return pl.pallas_call(
        functools.partial(_qkvconv_kernel, seq=l, tile=tile, heads_per_step=heads_per_step),
        grid=(b, n_blocks),
        in_specs=[
            pl.BlockSpec((None, l, width), lambda i, j: (i, 0, j)),
            pl.BlockSpec((SHORT_W, width), lambda i, j: (0, j)),
        ],
        out_specs=pl.BlockSpec((None, l, width), lambda i, j: (i, 0, j)),
        out_shape=jax.ShapeDtypeStruct((b, l, V_END), BF16),
        scratch_shapes=[pltpu.VMEM((l + 2 * QKV_PAD, width), F32)],
        compiler_params=pltpu.CompilerParams(
            dimension_semantics=("arbitrary", "arbitrary"), vmem_limit_bytes=VMEM_LIMIT),
        name="qkv_conv",
    )(p, conv_w)


def _delta_kernel(xf_ref, xb_ref, fcf_ref, fcb_ref, frf_ref, frb_ref, init_ref,
                  of_ref, ob_ref, st_ref, *, n_pos):
    @pl.when(pl.program_id(1) == 0)
    def _():
        st_ref[...] = init_ref[...]

    rr = lax.broadcasted_iota(jnp.int32, (CHUNK, CHUNK), 0)
    cc = lax.broadcasted_iota(jnp.int32, (CHUNK, CHUNK), 1)
    eye = (rr == cc).astype(F32)
    q_scale = HEAD_DIM ** -0.5

    class Chain:
        def __init__(self, x_ref, fc_ref, fr_ref, o_ref, c, h, forward):
            self.x_ref, self.fc_ref, self.fr_ref, self.o_ref = x_ref, fc_ref, fr_ref, o_ref
            self.c, self.h, self.forward = c, h, forward
            self.d = 0 if forward else 1
            self.s0 = pl.multiple_of(c * CHUNK, CHUNK)
            self.lo = h * HEAD_DIM

        def tile(self, which):
            off = which * D_DELTA + self.lo
            return self.x_ref[pl.ds(self.s0, CHUNK), off:off + HEAD_DIM].astype(F32)

        def cols(self):
            f = self.fc_ref[pl.ds(self.s0, CHUNK), :]
            ib = self.d * N_HEADS + self.h
            ig = (2 + self.d) * N_HEADS + self.h
            shape = (CHUNK, HEAD_DIM)
            return jnp.broadcast_to(f[:, ib:ib + 1], shape), jnp.broadcast_to(f[:, ig:ig + 1], shape)

        def g_row(self):
            ig = (2 + self.d) * N_HEADS + self.h
            return self.fr_ref[self.c, ig:ig + 1, :]

    def stage_gram(ch):
        ch.beta_b, ch.gc_b = ch.cols()
        k = ch.tile(1)
        ch.gram = _mm(jnp.concatenate([ch.tile(0) * q_scale, k * ch.beta_b], axis=0), k, NT_DIMS)
        incl = (rr >= cc) if ch.forward else (rr <= cc)
        ch.decay = jnp.where(incl, jnp.exp(jnp.where(incl, ch.gc_b[:, :CHUNK] - ch.g_row(), 0.0)), 0.0)

    def stage_power0(ch):
        strict = (rr > cc) if ch.forward else (rr < cc)
        ch.attn = (ch.gram[:CHUNK] * ch.decay).astype(BF16)
        neg = -jnp.where(strict, ch.gram[CHUNK:] * ch.decay, 0.0)
        ch.neg = neg.astype(BF16)
        ch.t_inv = eye + neg
        ch.pw = _mm(ch.neg, ch.neg)
        del ch.gram, ch.decay

    def stage_power(ch):
        both = _mm(jnp.concatenate([ch.pw, ch.t_inv], axis=0), ch.pw)
        ch.t_inv = ch.t_inv + both[CHUNK:]
        ch.pw = both[:CHUNK]

    def stage_power_last(ch):
        ch.t_inv = (ch.t_inv + _mm(ch.t_inv, ch.pw)).astype(BF16)
        del ch.pw

    def stage_newton_residual(ch):
        ch.err = eye - ch.t_inv.astype(F32) + _mm(ch.neg, ch.t_inv)
        del ch.neg

    def stage_newton_apply(ch):
        ch.t_inv = (ch.t_inv.astype(F32) + _mm(ch.t_inv, ch.err)).astype(BF16)
        del ch.err

    def stage_predict(ch):
        eg_b = jnp.exp(ch.gc_b)
        kbg = ch.tile(1) * (ch.beta_b * eg_b)
        qd = ch.tile(0) * (q_scale * eg_b)
        ps = _mm(jnp.concatenate([kbg, qd], axis=0), st_ref[ch.d, ch.h])
        ch.resid = ch.tile(2) * ch.beta_b - ps[:CHUNK]
        ch.qs = ps[CHUNK:]
        del ch.beta_b

    def stage_solve(ch):
        ch.v_new = _mm(ch.t_inv, ch.resid).astype(BF16)
        del ch.resid, ch.t_inv

    def stage_update(ch):
        g_row = ch.g_row()
        g_end = g_row[:, CHUNK - 1:CHUNK] if ch.forward else g_row[:, 0:1]
        ch.o_ref[pl.ds(ch.s0, CHUNK), ch.lo:ch.lo + HEAD_DIM] = (
            ch.qs + _mm(ch.attn, ch.v_new)).astype(ch.o_ref.dtype)
        k_dec = ch.tile(1) * jnp.exp(g_end - ch.gc_b)
        st_ref[ch.d, ch.h] = st_ref[ch.d, ch.h] * jnp.exp(g_end) + _mm(k_dec, ch.v_new, TN_DIMS)

    stages = ([stage_gram, stage_power0] + [stage_power] * 4
              + [stage_power_last, stage_newton_residual, stage_newton_apply,
                 stage_predict, stage_solve, stage_update])

    def body(c, carry):
        cb = n_pos - 1 - c
        chains = []
        for h in range(N_HEADS):
            chains.append(Chain(xf_ref, fcf_ref, frf_ref, of_ref, c, h, True))
            chains.append(Chain(xb_ref, fcb_ref, frb_ref, ob_ref, cb, h, False))
        for stage in stages:
            for ch in chains:
                stage(ch)
        return carry

    lax.fori_loop(0, n_pos, body, 0)


def _delta_call(qkv, fc, fr, init, *, lb):
    b, l, _ = qkv.shape
    nb = l // lb
    n_pos = lb // CHUNK
    fwd3 = lambda i, j: (i, j, 0)
    bwd3 = lambda i, j: (i, nb - 1 - j, 0)
    st_spec = pl.BlockSpec((None, 2, N_HEADS, HEAD_DIM, HEAD_DIM), lambda i, j: (i, 0, 0, 0, 0))
    return pl.pallas_call(
        functools.partial(_delta_kernel, n_pos=n_pos),
        grid=(b, nb),
        in_specs=[
            pl.BlockSpec((None, lb, V_END), fwd3),
            pl.BlockSpec((None, lb, V_END), bwd3),
            pl.BlockSpec((None, lb, LANES), fwd3),
            pl.BlockSpec((None, lb, LANES), bwd3),
            pl.BlockSpec((None, n_pos, N_FEAT, CHUNK), lambda i, j: (i, j, 0, 0)),
            pl.BlockSpec((None, n_pos, N_FEAT, CHUNK), lambda i, j: (i, nb - 1 - j, 0, 0)),
            st_spec,
        ],
        out_specs=[
            pl.BlockSpec((None, lb, D_DELTA), fwd3),
            pl.BlockSpec((None, lb, D_DELTA), bwd3),
            st_spec,
        ],
        out_shape=[
            jax.ShapeDtypeStruct((b, l, D_DELTA), BF16),
            jax.ShapeDtypeStruct((b, l, D_DELTA), BF16),
            jax.ShapeDtypeStruct((b, 2, N_HEADS, HEAD_DIM, HEAD_DIM), F32),
        ],
        compiler_params=pltpu.CompilerParams(
            dimension_semantics=("arbitrary", "arbitrary"), vmem_limit_bytes=VMEM_LIMIT),
        name="delta",
    )(qkv, qkv, fc, fc, fr, fr, init)


CONF_PAD = 16
CONF_CCHUNK = 128
SUBLANES = 8
CONF_SHIFT_ROWS = GRID_W + 2 * CONF_PAD - SUBLANES
CONF_UNROLL = 4


def _conformer_kernel(a_ref, g_ref, dw_ref, lnw_ref, lnb_ref, shift_ref, o_ref, pad_ref, sh_ref, cv_ref, *,
                      n_seg):
    zeros = jnp.zeros((CONF_PAD, D_CONV), F32)
    for slot in range(CONF_UNROLL):
        pad_ref[slot, 0:CONF_PAD, :] = zeros
        pad_ref[slot, CONF_PAD + GRID_W:2 * CONF_PAD + GRID_W, :] = zeros
    lnw = lnw_ref[...]
    lnb = lnb_ref[...]

    def realign(slot, r0):
        pad_ref[slot, CONF_PAD:CONF_PAD + GRID_W, :] = (
            a_ref[pl.ds(r0, GRID_W), :] * jax.nn.sigmoid(g_ref[pl.ds(r0, GRID_W), :]))
        padded = pad_ref[slot]
        hi = padded.astype(BF16)
        lo = (padded - hi.astype(F32)).astype(BF16)
        sh_ref[slot] = jnp.dot(shift_ref[...], jnp.concatenate([hi, lo], axis=0),
                               preferred_element_type=F32)

    def taps(slot, r0):
        for c0 in range(0, D_CONV, CONF_CCHUNK):
            cols = slice(c0, c0 + CONF_CCHUNK)
            acc = None
            for tap in range(CONF_W):
                start = CONF_PAD - CONF_W // 2 + tap
                r = start % SUBLANES
                base = start - r
                if r == 0:
                    win = pad_ref[slot, base:base + GRID_W, cols]
                else:
                    row = (r - 1) * CONF_SHIFT_ROWS + base
                    win = sh_ref[slot, row:row + GRID_W, cols]
                term = win * dw_ref[tap:tap + 1, cols]
                acc = term if acc is None else acc + term
            cv_ref[:, cols] = acc
        u = cv_ref[...]
        mu = jnp.mean(u, axis=-1, keepdims=True)
        uc = u - mu
        var = jnp.mean(uc * uc, axis=-1, keepdims=True)
        y = uc * lax.rsqrt(var + EPS) * lnw + lnb
        o_ref[pl.ds(r0, GRID_W), :] = _silu(y).astype(o_ref.dtype)

    def body(it, carry):
        starts = [pl.multiple_of((it * CONF_UNROLL + u) * GRID_W, GRID_W) for u in range(CONF_UNROLL)]
        for slot, r0 in enumerate(starts):
            realign(slot, r0)
        for slot, r0 in enumerate(starts):
            taps(slot, r0)
        return carry

    lax.fori_loop(0, n_seg // CONF_UNROLL, body, 0)


def _conformer_shift_matrix():
    padded_rows = GRID_W + 2 * CONF_PAD
    r = jnp.arange(1, SUBLANES)[:, None, None]
    i = jnp.arange(CONF_SHIFT_ROWS)[None, :, None]
    col = jnp.arange(2 * padded_rows)[None, None, :]
    hit = (col == i + r) | (col == padded_rows + i + r)
    return hit.reshape((SUBLANES - 1) * CONF_SHIFT_ROWS, 2 * padded_rows).astype(BF16)


def _conformer_call(p2, dw, lnw, lnb, *, tt):
    m = p2.shape[0]
    a_blk = (V_END + D_DELTA) // D_CONV
    shift = _conformer_shift_matrix()
    return pl.pallas_call(
        functools.partial(_conformer_kernel, n_seg=tt // GRID_W),
        grid=(m // tt,),
        in_specs=[
            pl.BlockSpec((tt, D_CONV), lambda i: (i, a_blk)),
            pl.BlockSpec((tt, D_CONV), lambda i: (i, a_blk + 1)),
            pl.BlockSpec((CONF_W, D_CONV), lambda i: (0, 0)),
            pl.BlockSpec((1, D_CONV), lambda i: (0, 0)),
            pl.BlockSpec((1, D_CONV), lambda i: (0, 0)),
            pl.BlockSpec(shift.shape, lambda i: (0, 0)),
        ],
        out_specs=pl.BlockSpec((tt, D_CONV), lambda i: (i, 0)),
        out_shape=jax.ShapeDtypeStruct((m, D_CONV), BF16),
        scratch_shapes=[
            pltpu.VMEM((CONF_UNROLL, GRID_W + 2 * CONF_PAD, D_CONV), F32),
            pltpu.VMEM((CONF_UNROLL, (SUBLANES - 1) * CONF_SHIFT_ROWS, D_CONV), F32),
            pltpu.VMEM((GRID_W, D_CONV), F32),
        ],
        compiler_params=pltpu.CompilerParams(
            dimension_semantics=("arbitrary",), vmem_limit_bytes=VMEM_LIMIT),
        name="conformer",
    )(p2, p2, dw, lnw, lnb, shift)


def _outproj_kernel(of_ref, ob_ref, z_ref, dnw_ref, mc_ref, w1_ref, w2_ref, x_ref, g_ref, nw_ref, o_ref, *,
                    tiles_per_batch):
    b = pl.program_id(0) // tiles_per_batch
    dnw = dnw_ref[...]
    heads = []
    for h in range(N_HEADS):
        lo = h * HEAD_DIM
        oh = of_ref[:, lo:lo + HEAD_DIM].astype(F32) + ob_ref[:, lo:lo + HEAD_DIM].astype(F32)
        heads.append((_rms(oh, dnw) * _silu(z_ref[:, lo:lo + HEAD_DIM])).astype(BF16))
    md = jnp.concatenate(heads, axis=1)
    y = (jnp.dot(md, w1_ref[...], preferred_element_type=F32)
         + jnp.dot(mc_ref[...], w2_ref[...], preferred_element_type=F32))
    g = g_ref[pl.ds(b, 1), :]
    o_ref[...] = x_ref[...] + g * _rms(y, nw_ref[...])


def _outproj_call(o_f, o_b, p2, dnw, mc, w_out, x2, mod, norm_w, *, tm, tiles_per_batch):
    m = x2.shape[0]
    z_blk = V_END // D_DELTA
    return pl.pallas_call(
        functools.partial(_outproj_kernel, tiles_per_batch=tiles_per_batch),
        grid=(m // tm,),
        in_specs=[
            pl.BlockSpec((tm, D_DELTA), lambda i: (i, 0)),
            pl.BlockSpec((tm, D_DELTA), lambda i: (i, 0)),
            pl.BlockSpec((tm, D_DELTA), lambda i: (i, z_blk)),
            pl.BlockSpec((1, HEAD_DIM), lambda i: (0, 0)),
            pl.BlockSpec((tm, D_CONV), lambda i: (i, 0)),
            pl.BlockSpec((D_DELTA, D_MODEL), lambda i: (0, 0)),
            pl.BlockSpec((D_CONV, D_MODEL), lambda i: (1, 0)),
            pl.BlockSpec((tm, D_MODEL), lambda i: (i, 0)),
            pl.BlockSpec((8, D_MODEL), lambda i: (0, 2)),
            pl.BlockSpec((1, D_MODEL), lambda i: (0, 0)),
        ],
        out_specs=pl.BlockSpec((tm, D_MODEL), lambda i: (i, 0)),
        out_shape=jax.ShapeDtypeStruct((m, D_MODEL), F32),
        compiler_params=pltpu.CompilerParams(
            dimension_semantics=("arbitrary",), vmem_limit_bytes=VMEM_LIMIT),
        name="out_proj",
    )(o_f, o_b, p2, dnw, mc, w_out, w_out, x2, mod, norm_w)


def _ffn_kernel(x_ref, nw_ref, sh_ref, sc_ref, g_ref, pw_ref, wg_ref, wu_ref, wd_ref, o_ref, h_ref, *,
                tiles_per_batch, n_k):
    i = pl.program_id(0)
    k = pl.program_id(1)
    b = i // tiles_per_batch

    @pl.when(k == 0)
    def _():
        sh = sh_ref[pl.ds(b, 1), :]
        sc = sc_ref[pl.ds(b, 1), :]
        h_ref[...] = (_rms(x_ref[...], nw_ref[...]) * (1.0 + sc) + sh).astype(BF16)
        o_ref[...] = jnp.zeros(o_ref.shape, F32)

    hh = h_ref[...]
    gate = jnp.dot(hh, wg_ref[...], preferred_element_type=F32)
    up = jnp.dot(hh, wu_ref[...], preferred_element_type=F32)
    act = (_silu(gate) * up).astype(BF16)
    o_ref[...] += jnp.dot(act, wd_ref[...], preferred_element_type=F32)

    @pl.when(k == n_k - 1)
    def _():
        g = g_ref[pl.ds(b, 1), :]
        o_ref[...] = x_ref[...] + g * _rms(o_ref[...], pw_ref[...])


def _ffn_call(x2, mod, norm_pre, norm_post, wg, wu, wd, *, tm, tf, tiles_per_batch):
    m = x2.shape[0]
    dff = wg.shape[1]
    n_k = dff // tf
    return pl.pallas_call(
        functools.partial(_ffn_kernel, tiles_per_batch=tiles_per_batch, n_k=n_k),
        grid=(m // tm, n_k),
        in_specs=[
            pl.BlockSpec((tm, D_MODEL), lambda i, k: (i, 0)),
            pl.BlockSpec((1, D_MODEL), lambda i, k: (0, 0)),
            pl.BlockSpec((8, D_MODEL), lambda i, k: (0, 3)),
            pl.BlockSpec((8, D_MODEL), lambda i, k: (0, 4)),
            pl.BlockSpec((8, D_MODEL), lambda i, k: (0, 5)),
            pl.BlockSpec((1, D_MODEL), lambda i, k: (0, 0)),
            pl.BlockSpec((D_MODEL, tf), lambda i, k: (0, k)),
            pl.BlockSpec((D_MODEL, tf), lambda i, k: (0, k)),
            pl.BlockSpec((tf, D_MODEL), lambda i, k: (k, 0)),
        ],
        out_specs=pl.BlockSpec((tm, D_MODEL), lambda i, k: (i, 0)),
        out_shape=jax.ShapeDtypeStruct((m, D_MODEL), F32),
        scratch_shapes=[pltpu.VMEM((tm, D_MODEL), BF16)],
        compiler_params=pltpu.CompilerParams(
            dimension_semantics=("arbitrary", "arbitrary"), vmem_limit_bytes=VMEM_LIMIT),
        name="ffn",
    )(x2, norm_pre, mod, mod, mod, norm_post, wg, wu, wd)


@jax.jit
def _forward(x, c, ctx, c_ctx, w_mod, b_mod, mix_norm_pre, mix_norm_post, w_in, qkv_conv,
             a_log, dt_bias, delta_out_norm, conf_dw, conf_ln_w, conf_ln_b, w_out,
             ffn_norm_pre, ffn_norm_post, w_gate, w_up, w_down):
    bsz, seq, d = x.shape
    ctx_len = ctx.shape[1]
    assert d == D_MODEL and w_mod.shape[0] == 1, "single-layer kernel"
    assert seq % 1024 == 0 and ctx_len % CHUNK == 0 and bsz + 1 <= 8

    cc = jnp.zeros((8, D_MODEL), F32).at[:bsz].set(c).at[bsz].set(c_ctx)
    mod = _mod_call(cc, w_mod[0], b_mod[0][None, :])

    w_in0 = w_in[0]
    w_main = jnp.concatenate([w_in0[:, :V_END].astype(BF16), w_in0[:, STATE_END:].astype(BF16)], axis=1)
    w_g = jnp.zeros((D_MODEL, LANES), BF16).at[:, :N_FEAT].set(w_in0[:, V_END:STATE_END].astype(BF16))
    prm = (jnp.zeros((8, LANES), F32)
           .at[0, 2 * N_HEADS:N_FEAT].set(a_log[0].reshape(-1))
           .at[1, 2 * N_HEADS:N_FEAT].set(dt_bias[0].reshape(-1)))

    x2 = x.reshape(bsz * seq, D_MODEL)
    ctx2 = ctx.reshape(bsz * ctx_len, D_MODEL)
    t = _TILES
    p, ba = _inproj_call(x2, mix_norm_pre, mod, w_main, w_g, n_cols=D_MAIN, tm=t.inproj_tm, tn=t.inproj_tn,
                         tiles_per_mod_row=seq // t.inproj_tm, mod_row0=0)
    pc, bac = _inproj_call(ctx2, mix_norm_pre, mod, w_main, w_g, n_cols=V_END, tm=bsz * ctx_len,
                           tn=t.inproj_tn, tiles_per_mod_row=1, mod_row0=bsz)

    fc, fr = _gatefeat_call(ba.reshape(bsz, seq, LANES), prm)
    fc_c, fr_c = _gatefeat_call(bac.reshape(bsz, ctx_len, LANES), prm)
    qkv = _qkvconv_call(p.reshape(bsz, seq, D_MAIN), qkv_conv[0], tile=t.qkv_tile, heads_per_step=1)
    qkv_c = _qkvconv_call(pc.reshape(bsz, ctx_len, V_END), qkv_conv[0], tile=t.qkv_tile,
                          heads_per_step=N_HEADS)

    zero_state = jnp.zeros((bsz, 2, N_HEADS, HEAD_DIM, HEAD_DIM), F32)
    _, _, s_ctx = _delta_call(qkv_c, fc_c, fr_c, zero_state, lb=ctx_len)
    o_f, o_b, _ = _delta_call(qkv, fc, fr, s_ctx, lb=t.delta_block)

    mix_c = _conformer_call(p, conf_dw[0], conf_ln_w, conf_ln_b, tt=t.conf_rows)

    x1 = _outproj_call(o_f.reshape(bsz * seq, D_DELTA), o_b.reshape(bsz * seq, D_DELTA), p, delta_out_norm,
                       mix_c, w_out[0].astype(BF16), x2, mod, mix_norm_post,
                       tm=t.outproj_tm, tiles_per_batch=seq // t.outproj_tm)
    out = _ffn_call(x1, mod, ffn_norm_pre, ffn_norm_post, w_gate[0].astype(BF16), w_up[0].astype(BF16),
                    w_down[0].astype(BF16), tm=t.ffn_tm, tf=t.ffn_tf, tiles_per_batch=seq // t.ffn_tm)
    return out.reshape(bsz, seq, D_MODEL)


def kernel(x, c, ctx, c_ctx, w_mod, b_mod, mix_norm_pre, mix_norm_post, w_in, qkv_conv, a_log, dt_bias,
           delta_out_norm, conf_dw, conf_ln_w, conf_ln_b, w_out, ffn_norm_pre, ffn_norm_post,
           w_gate, w_up, w_down):
    return _forward(x, c, ctx, c_ctx, w_mod, b_mod, mix_norm_pre, mix_norm_post, w_in, qkv_conv,
                    a_log, dt_bias, delta_out_norm, conf_dw, conf_ln_w, conf_ln_b, w_out,
                    ffn_norm_pre, ffn_norm_post, w_gate, w_up, w_down)
```

```python
import functools
import math
from typing import NamedTuple

import jax
import jax.numpy as jnp
from jax import lax
from jax.experimental import pallas as pl
from jax.experimental.pallas import tpu as pltpu

F32 = jnp.float32
BF16 = jnp.bfloat16
HIGHEST = lax.Precision.HIGHEST

D_MODEL = 2048
N_HEADS = 8
HEAD_DIM = 128
D_DELTA = N_HEADS * HEAD_DIM
D_CONV = D_MODEL - D_DELTA
GRID_W = 64
CHUNK = 64
SHORT_W = 5
CONF_W = 31
EPS = 1e-6
V_END = 3 * D_DELTA
STATE_END = V_END + 4 * N_HEADS
Z_END = STATE_END + D_DELTA
D_MAIN = V_END + D_DELTA + 2 * D_CONV
N_FEAT = 4 * N_HEADS
LANES = 128
VMEM_LIMIT = 56 * 1024 * 1024


class _Tiles(NamedTuple):
    inproj_tm: int = 1024
    inproj_tn: int = 1536
    qkv_tile: int = 256
    delta_block: int = 512
    conf_rows: int = 512
    outproj_tm: int = 512
    ffn_tm: int = 1024
    ffn_tf: int = 512


_TILES = _Tiles()

NT_DIMS = (((1,), (1,)), ((), ()))
TN_DIMS = (((0,), (0,)), ((), ()))


def _mm(a, b, dims=None):
    a = a.astype(BF16)
    b = b.astype(BF16)
    if dims is None:
        return jnp.dot(a, b, preferred_element_type=F32)
    return lax.dot_general(a, b, dims, preferred_element_type=F32)


def _silu(x):
    return x * jax.nn.sigmoid(x)


def _rms(x, w):
    return x * lax.rsqrt(jnp.mean(x * x, axis=-1, keepdims=True) + EPS) * w


def _mod_kernel(c_ref, w_ref, b_ref, o_ref):
    s = _silu(c_ref[...])
    s1 = s.astype(BF16)
    s2 = (s - s1.astype(F32)).astype(BF16)
    w = w_ref[...]
    w_hi = w.astype(BF16)
    w_lo = (w - w_hi.astype(F32)).astype(BF16)
    rows = s.shape[0]
    p_hi = jnp.dot(jnp.concatenate([s1, s2], axis=0), w_hi, preferred_element_type=F32)
    p_lo = jnp.dot(s1, w_lo, preferred_element_type=F32)
    o_ref[...] = p_hi[:rows] + p_hi[rows:] + p_lo + b_ref[...]


def _mod_call(cc, w_mod, b_mod):
    n = w_mod.shape[1]
    tn = 1024
    return pl.pallas_call(
        _mod_kernel,
        grid=(n // tn,),
        in_specs=[
            pl.BlockSpec((8, D_MODEL), lambda j: (0, 0)),
            pl.BlockSpec((D_MODEL, tn), lambda j: (0, j)),
            pl.BlockSpec((1, tn), lambda j: (0, j)),
        ],
        out_specs=pl.BlockSpec((8, tn), lambda j: (0, j)),
        out_shape=jax.ShapeDtypeStruct((8, n), F32),
        compiler_params=pltpu.CompilerParams(
            dimension_semantics=("arbitrary",), vmem_limit_bytes=VMEM_LIMIT),
        name="mod",
    )(cc, w_mod, b_mod)


def _inproj_kernel(x_ref, nw_ref, sh_ref, sc_ref, w_ref, wg_ref, o_ref, og_ref, h_ref, *,
                   tiles_per_mod_row, mod_row0):
    i = pl.program_id(0)
    j = pl.program_id(1)

    @pl.when(j == 0)
    def _():
        r = mod_row0 + i // tiles_per_mod_row
        sh = sh_ref[pl.ds(r, 1), :]
        sc = sc_ref[pl.ds(r, 1), :]
        hh = (_rms(x_ref[...], nw_ref[...]) * (1.0 + sc) + sh).astype(BF16)
        h_ref[...] = hh
        og_ref[...] = jnp.dot(hh, wg_ref[...], preferred_element_type=F32)

    o_ref[...] = jnp.dot(h_ref[...], w_ref[...], preferred_element_type=F32)


def _inproj_call(x2, norm_w, mod, w_main, w_g, *, n_cols, tm, tn, tiles_per_mod_row, mod_row0):
    m = x2.shape[0]
    kern = functools.partial(_inproj_kernel, tiles_per_mod_row=tiles_per_mod_row, mod_row0=mod_row0)
    return pl.pallas_call(
        kern,
        grid=(m // tm, n_cols // tn),
        in_specs=[
            pl.BlockSpec((tm, D_MODEL), lambda i, j: (i, 0)),
            pl.BlockSpec((1, D_MODEL), lambda i, j: (0, 0)),
            pl.BlockSpec((8, D_MODEL), lambda i, j: (0, 0)),
            pl.BlockSpec((8, D_MODEL), lambda i, j: (0, 1)),
            pl.BlockSpec((D_MODEL, tn), lambda i, j: (0, j)),
            pl.BlockSpec((D_MODEL, LANES), lambda i, j: (0, 0)),
        ],
        out_specs=[
            pl.BlockSpec((tm, tn), lambda i, j: (i, j)),
            pl.BlockSpec((tm, LANES), lambda i, j: (i, 0)),
        ],
        out_shape=[
            jax.ShapeDtypeStruct((m, n_cols), F32),
            jax.ShapeDtypeStruct((m, LANES), F32),
        ],
        scratch_shapes=[pltpu.VMEM((tm, D_MODEL), BF16)],
        compiler_params=pltpu.CompilerParams(
            dimension_semantics=("arbitrary", "arbitrary"), vmem_limit_bytes=VMEM_LIMIT),
        name="in_proj",
    )(x2, norm_w, mod, mod, w_main, w_g)


GATE_UNROLL_MAX = 8


def _gatefeat_kernel(ba_ref, prm_ref, fc_ref, fr_ref, *, n_chunks, unroll):
    lane = lax.broadcasted_iota(jnp.int32, (CHUNK, LANES), 1)
    rr = lax.broadcasted_iota(jnp.int32, (CHUNK, CHUNK), 0)
    cc = lax.broadcasted_iota(jnp.int32, (CHUNK, CHUNK), 1)
    ltri = (rr >= cc).astype(F32)
    utri = (rr <= cc).astype(F32)
    r2 = lax.broadcasted_iota(jnp.int32, (LANES, LANES), 0)
    c2 = lax.broadcasted_iota(jnp.int32, (LANES, LANES), 1)
    eye = (r2 == c2).astype(F32)
    neg_a = -jnp.exp(prm_ref[0:1, :])
    dtb = prm_ref[1:2, :]

    def body(it, carry):
        idx = [it * unroll + u for u in range(unroll)]
        starts = [pl.multiple_of(ci * CHUNK, CHUNK) for ci in idx]
        xs = [ba_ref[pl.ds(s0, CHUNK), :] for s0 in starts]
        gs = []
        for x in xs:
            y = x + dtb
            gs.append(neg_a * (jnp.maximum(y, 0.0) + jnp.log1p(jnp.exp(-jnp.abs(y)))))
        pres = [jnp.dot(ltri, g, preferred_element_type=F32, precision=HIGHEST) for g in gs]
        sufs = [jnp.dot(utri, g, preferred_element_type=F32, precision=HIGHEST) for g in gs]
        fs = [jnp.where(lane < 2 * N_HEADS, jax.nn.sigmoid(x), jnp.where(lane < 3 * N_HEADS, pre, suf))
              for x, pre, suf in zip(xs, pres, sufs)]
        fts = [lax.dot_general(eye, f, NT_DIMS, preferred_element_type=F32, precision=HIGHEST) for f in fs]
        for ci, s0, f, ft in zip(idx, starts, fs, fts):
            fc_ref[pl.ds(s0, CHUNK), :] = f
            fr_ref[ci] = ft[0:N_FEAT, :]
        return carry

    lax.fori_loop(0, n_chunks // unroll, body, 0)


def _gatefeat_call(ba, prm):
    b, l, _ = ba.shape
    n_chunks = l // CHUNK
    unroll = math.gcd(n_chunks, GATE_UNROLL_MAX)
    return pl.pallas_call(
        functools.partial(_gatefeat_kernel, n_chunks=n_chunks, unroll=unroll),
        grid=(b,),
        in_specs=[
            pl.BlockSpec((None, l, LANES), lambda i: (i, 0, 0)),
            pl.BlockSpec((8, LANES), lambda i: (0, 0)),
        ],
        out_specs=[
            pl.BlockSpec((None, l, LANES), lambda i: (i, 0, 0)),
            pl.BlockSpec((None, n_chunks, N_FEAT, CHUNK), lambda i: (i, 0, 0, 0)),
        ],
        out_shape=[
            jax.ShapeDtypeStruct((b, l, LANES), F32),
            jax.ShapeDtypeStruct((b, n_chunks, N_FEAT, CHUNK), F32),
        ],
        compiler_params=pltpu.CompilerParams(
            dimension_semantics=("arbitrary",), vmem_limit_bytes=VMEM_LIMIT),
        name="gate_feat",
    )(ba, prm)


QKV_PAD = 8


def _qkvconv_kernel(x_ref, w_ref, o_ref, pad_ref, *, seq, tile, heads_per_step):
    j = pl.program_id(1)
    width = heads_per_step * HEAD_DIM
    zeros = jnp.zeros((QKV_PAD, width), F32)
    pad_ref[0:QKV_PAD, :] = zeros
    pad_ref[QKV_PAD + seq:2 * QKV_PAD + seq, :] = zeros
    pad_ref[QKV_PAD:QKV_PAD + seq, :] = x_ref[...]
    w = w_ref[...]
    for s in range(heads_per_step):
        cols = slice(s * HEAD_DIM, (s + 1) * HEAD_DIM)
        is_qk = j * heads_per_step + s < 2 * N_HEADS
        for t0 in range(0, seq, tile):
            acc = None
            for tap in range(SHORT_W):
                start = t0 + QKV_PAD - SHORT_W // 2 + tap
                term = pad_ref[start:start + tile, cols] * w[tap:tap + 1, cols]
                acc = term if acc is None else acc + term
            y = _silu(acc)
            inv = lax.rsqrt(jnp.sum(y * y, axis=-1, keepdims=True) + EPS)
            o_ref[t0:t0 + tile, cols] = (y * jnp.where(is_qk, inv, 1.0)).astype(o_ref.dtype)


def _qkvconv_call(p, conv_w, *, tile, heads_per_step):
    b, l, _ = p.shape
    width = heads_per_step * HEAD_DIM
    n_blocks = V_END // width
    return pl.pallas_call(
        functools.partial(_qkvconv_kernel, seq=l, tile=tile, heads_per_step=heads_per_step),
        grid=(b, n_blocks),
        in_specs=[
            pl.BlockSpec((None, l, width), lambda i, j: (i, 0, j)),
            pl.BlockSpec((SHORT_W, width), lambda i, j: (0, j)),
        ],
        out_specs=pl.BlockSpec((None, l, width), lambda i, j: (i, 0, j)),
        out_shape=jax.ShapeDtypeStruct((b, l, V_END), BF16),
        scratch_shapes=[pltpu.VMEM((l + 2 * QKV_PAD, width), F32)],
        compiler_params=pltpu.CompilerParams(
            dimension_semantics=("arbitrary", "arbitrary"), vmem_limit_bytes=VMEM_LIMIT),
        name="qkv_conv",
    )(p, conv_w)


def _delta_kernel(xf_ref, xb_ref, fcf_ref, fcb_ref, frf_ref, frb_ref, init_ref,
                  of_ref, ob_ref, st_ref, *, n_pos):
    @pl.when(pl.program_id(1) == 0)
    def _():
        st_ref[...] = init_ref[...]

    rr = lax.broadcasted_iota(jnp.int32, (CHUNK, CHUNK), 0)
    cc = lax.broadcasted_iota(jnp.int32, (CHUNK, CHUNK), 1)
    eye = (rr == cc).astype(F32)
    q_scale = HEAD_DIM ** -0.5

    class Chain:
        def __init__(self, x_ref, fc_ref, fr_ref, o_ref, c, h, forward):
            self.x_ref, self.fc_ref, self.fr_ref, self.o_ref = x_ref, fc_ref, fr_ref, o_ref
            self.c, self.h, self.forward = c, h, forward
            self.d = 0 if forward else 1
            self.s0 = pl.multiple_of(c * CHUNK, CHUNK)
            self.lo = h * HEAD_DIM

        def tile(self, which):
            off = which * D_DELTA + self.lo
            return self.x_ref[pl.ds(self.s0, CHUNK), off:off + HEAD_DIM].astype(F32)

        def cols(self):
            f = self.fc_ref[pl.ds(self.s0, CHUNK), :]
            ib = self.d * N_HEADS + self.h
            ig = (2 + self.d) * N_HEADS + self.h
            shape = (CHUNK, HEAD_DIM)
            return jnp.broadcast_to(f[:, ib:ib + 1], shape), jnp.broadcast_to(f[:, ig:ig + 1], shape)

        def g_row(self):
            ig = (2 + self.d) * N_HEADS + self.h
            return self.fr_ref[self.c, ig:ig + 1, :]

    def stage_gram(ch):
        ch.beta_b, ch.gc_b = ch.cols()
        k = ch.tile(1)
        ch.gram = _mm(jnp.concatenate([ch.tile(0) * q_scale, k * ch.beta_b], axis=0), k, NT_DIMS)
        incl = (rr >= cc) if ch.forward else (rr <= cc)
        ch.decay = jnp.where(incl, jnp.exp(jnp.where(incl, ch.gc_b[:, :CHUNK] - ch.g_row(), 0.0)), 0.0)

    def stage_power0(ch):
        strict = (rr > cc) if ch.forward else (rr < cc)
        ch.attn = (ch.gram[:CHUNK] * ch.decay).astype(BF16)
        neg = -jnp.where(strict, ch.gram[CHUNK:] * ch.decay, 0.0)
        ch.neg = neg.astype(BF16)
        ch.t_inv = eye + neg
        ch.pw = _mm(ch.neg, ch.neg)
        del ch.gram, ch.decay

    def stage_power(ch):
        both = _mm(jnp.concatenate([ch.pw, ch.t_inv], axis=0), ch.pw)
        ch.t_inv = ch.t_inv + both[CHUNK:]
        ch.pw = both[:CHUNK]

    def stage_power_last(ch):
        ch.t_inv = (ch.t_inv + _mm(ch.t_inv, ch.pw)).astype(BF16)
        del ch.pw

    def stage_newton_residual(ch):
        ch.err = eye - ch.t_inv.astype(F32) + _mm(ch.neg, ch.t_inv)
        del ch.neg

    def stage_newton_apply(ch):
        ch.t_inv = (ch.t_inv.astype(F32) + _mm(ch.t_inv, ch.err)).astype(BF16)
        del ch.err

    def stage_predict(ch):
        eg_b = jnp.exp(ch.gc_b)
        kbg = ch.tile(1) * (ch.beta_b * eg_b)
        qd = ch.tile(0) * (q_scale * eg_b)
        ps = _mm(jnp.concatenate([kbg, qd], axis=0), st_ref[ch.d, ch.h])
        ch.resid = ch.tile(2) * ch.beta_b - ps[:CHUNK]
        ch.qs = ps[CHUNK:]
        del ch.beta_b

    def stage_solve(ch):
        ch.v_new = _mm(ch.t_inv, ch.resid).astype(BF16)
        del ch.resid, ch.t_inv

    def stage_update(ch):
        g_row = ch.g_row()
        g_end = g_row[:, CHUNK - 1:CHUNK] if ch.forward else g_row[:, 0:1]
        ch.o_ref[pl.ds(ch.s0, CHUNK), ch.lo:ch.lo + HEAD_DIM] = (
            ch.qs + _mm(ch.attn, ch.v_new)).astype(ch.o_ref.dtype)
        k_dec = ch.tile(1) * jnp.exp(g_end - ch.gc_b)
        st_ref[ch.d, ch.h] = st_ref[ch.d, ch.h] * jnp.exp(g_end) + _mm(k_dec, ch.v_new, TN_DIMS)

    stages = ([stage_gram, stage_power0] + [stage_power] * 4
              + [stage_power_last, stage_newton_residual, stage_newton_apply,
                 stage_predict, stage_solve, stage_update])

    def body(c, carry):
        cb = n_pos - 1 - c
        chains = []
        for h in range(N_HEADS):
            chains.append(Chain(xf_ref, fcf_ref, frf_ref, of_ref, c, h, True))
            chains.append(Chain(xb_ref, fcb_ref, frb_ref, ob_ref, cb, h, False))
        for stage in stages:
            for ch in chains:
                stage(ch)
        return carry

    lax.fori_loop(0, n_pos, body, 0)


def _delta_call(qkv, fc, fr, init, *, lb):
    b, l, _ = qkv.shape
    nb = l // lb
    n_pos = lb // CHUNK
    fwd3 = lambda i, j: (i, j, 0)
    bwd3 = lambda i, j: (i, nb - 1 - j, 0)
    st_spec = pl.BlockSpec((None, 2, N_HEADS, HEAD_DIM, HEAD_DIM), lambda i, j: (i, 0, 0, 0, 0))
    return pl.pallas_call(
        functools.partial(_delta_kernel, n_pos=n_pos),
        grid=(b, nb),
        in_specs=[
            pl.BlockSpec((None, lb, V_END), fwd3),
            pl.BlockSpec((None, lb, V_END), bwd3),
            pl.BlockSpec((None, lb, LANES), fwd3),
            pl.BlockSpec((None, lb, LANES), bwd3),
            pl.BlockSpec((None, n_pos, N_FEAT, CHUNK), lambda i, j: (i, j, 0, 0)),
            pl.BlockSpec((None, n_pos, N_FEAT, CHUNK), lambda i, j: (i, nb - 1 - j, 0, 0)),
            st_spec,
        ],
        out_specs=[
            pl.BlockSpec((None, lb, D_DELTA), fwd3),
            pl.BlockSpec((None, lb, D_DELTA), bwd3),
            st_spec,
        ],
        out_shape=[
            jax.ShapeDtypeStruct((b, l, D_DELTA), BF16),
            jax.ShapeDtypeStruct((b, l, D_DELTA), BF16),
            jax.ShapeDtypeStruct((b, 2, N_HEADS, HEAD_DIM, HEAD_DIM), F32),
        ],
        compiler_params=pltpu.CompilerParams(
            dimension_semantics=("arbitrary", "arbitrary"), vmem_limit_bytes=VMEM_LIMIT),
        name="delta",
    )(qkv, qkv, fc, fc, fr, fr, init)


CONF_PAD = 16
CONF_CCHUNK = 128
SUBLANES = 8
CONF_SHIFT_ROWS = GRID_W + 2 * CONF_PAD - SUBLANES
CONF_UNROLL = 4


def _conformer_kernel(a_ref, g_ref, dw_ref, lnw_ref, lnb_ref, shift_ref, o_ref, pad_ref, sh_ref, cv_ref, *,
                      n_seg):
    zeros = jnp.zeros((CONF_PAD, D_CONV), F32)
    for slot in range(CONF_UNROLL):
        pad_ref[slot, 0:CONF_PAD, :] = zeros
        pad_ref[slot, CONF_PAD + GRID_W:2 * CONF_PAD + GRID_W, :] = zeros
    lnw = lnw_ref[...]
    lnb = lnb_ref[...]

    def realign(slot, r0):
        pad_ref[slot, CONF_PAD:CONF_PAD + GRID_W, :] = (
            a_ref[pl.ds(r0, GRID_W), :] * jax.nn.sigmoid(g_ref[pl.ds(r0, GRID_W), :]))
        padded = pad_ref[slot]
        hi = padded.astype(BF16)
        lo = (padded - hi.astype(F32)).astype(BF16)
        sh_ref[slot] = jnp.dot(shift_ref[...], jnp.concatenate([hi, lo], axis=0),
                               preferred_element_type=F32)

    def taps(slot, r0):
        for c0 in range(0, D_CONV, CONF_CCHUNK):
            cols = slice(c0, c0 + CONF_CCHUNK)
            acc = None
            for tap in range(CONF_W):
                start = CONF_PAD - CONF_W // 2 + tap
                r = start % SUBLANES
                base = start - r
                if r == 0:
                    win = pad_ref[slot, base:base + GRID_W, cols]
                else:
                    row = (r - 1) * CONF_SHIFT_ROWS + base
                    win = sh_ref[slot, row:row + GRID_W, cols]
                term = win * dw_ref[tap:tap + 1, cols]
                acc = term if acc is None else acc + term
            cv_ref[:, cols] = acc
        u = cv_ref[...]
        mu = jnp.mean(u, axis=-1, keepdims=True)
        uc = u - mu
        var = jnp.mean(uc * uc, axis=-1, keepdims=True)
        y = uc * lax.rsqrt(var + EPS) * lnw + lnb
        o_ref[pl.ds(r0, GRID_W), :] = _silu(y).astype(o_ref.dtype)

    def body(it, carry):
        starts = [pl.multiple_of((it * CONF_UNROLL + u) * GRID_W, GRID_W) for u in range(CONF_UNROLL)]
        for slot, r0 in enumerate(starts):
            realign(slot, r0)
        for slot, r0 in enumerate(starts):
            taps(slot, r0)
        return carry

    lax.fori_loop(0, n_seg // CONF_UNROLL, body, 0)


def _conformer_shift_matrix():
    padded_rows = GRID_W + 2 * CONF_PAD
    r = jnp.arange(1, SUBLANES)[:, None, None]
    i = jnp.arange(CONF_SHIFT_ROWS)[None, :, None]
    col = jnp.arange(2 * padded_rows)[None, None, :]
    hit = (col == i + r) | (col == padded_rows + i + r)
    return hit.reshape((SUBLANES - 1) * CONF_SHIFT_ROWS, 2 * padded_rows).astype(BF16)


def _conformer_call(p2, dw, lnw, lnb, *, tt):
    m = p2.shape[0]
    a_blk = (V_END + D_DELTA) // D_CONV
    shift = _conformer_shift_matrix()
    return pl.pallas_call(
        functools.partial(_conformer_kernel, n_seg=tt // GRID_W),
        grid=(m // tt,),
        in_specs=[
            pl.BlockSpec((tt, D_CONV), lambda i: (i, a_blk)),
            pl.BlockSpec((tt, D_CONV), lambda i: (i, a_blk + 1)),
            pl.BlockSpec((CONF_W, D_CONV), lambda i: (0, 0)),
            pl.BlockSpec((1, D_CONV), lambda i: (0, 0)),
            pl.BlockSpec((1, D_CONV), lambda i: (0, 0)),
            pl.BlockSpec(shift.shape, lambda i: (0, 0)),
        ],
        out_specs=pl.BlockSpec((tt, D_CONV), lambda i: (i, 0)),
        out_shape=jax.ShapeDtypeStruct((m, D_CONV), BF16),
        scratch_shapes=[
            pltpu.VMEM((CONF_UNROLL, GRID_W + 2 * CONF_PAD, D_CONV), F32),
            pltpu.VMEM((CONF_UNROLL, (SUBLANES - 1) * CONF_SHIFT_ROWS, D_CONV), F32),
            pltpu.VMEM((GRID_W, D_CONV), F32),
        ],
        compiler_params=pltpu.CompilerParams(
            dimension_semantics=("arbitrary",), vmem_limit_bytes=VMEM_LIMIT),
        name="conformer",
    )(p2, p2, dw, lnw, lnb, shift)


def _outproj_kernel(of_ref, ob_ref, z_ref, dnw_ref, mc_ref, w1_ref, w2_ref, x_ref, g_ref, nw_ref, o_ref, *,
                    tiles_per_batch):
    b = pl.program_id(0) // tiles_per_batch
    dnw = dnw_ref[...]
    heads = []
    for h in range(N_HEADS):
        lo = h * HEAD_DIM
        oh = of_ref[:, lo:lo + HEAD_DIM].astype(F32) + ob_ref[:, lo:lo + HEAD_DIM].astype(F32)
        heads.append((_rms(oh, dnw) * _silu(z_ref[:, lo:lo + HEAD_DIM])).astype(BF16))
    md = jnp.concatenate(heads, axis=1)
    y = (jnp.dot(md, w1_ref[...], preferred_element_type=F32)
         + jnp.dot(mc_ref[...], w2_ref[...], preferred_element_type=F32))
    g = g_ref[pl.ds(b, 1), :]
    o_ref[...] = x_ref[...] + g * _rms(y, nw_ref[...])


def _outproj_call(o_f, o_b, p2, dnw, mc, w_out, x2, mod, norm_w, *, tm, tiles_per_batch):
    m = x2.shape[0]
    z_blk = V_END // D_DELTA
    return pl.pallas_call(
        functools.partial(_outproj_kernel, tiles_per_batch=tiles_per_batch),
        grid=(m // tm,),
        in_specs=[
            pl.BlockSpec((tm, D_DELTA), lambda i: (i, 0)),
            pl.BlockSpec((tm, D_DELTA), lambda i: (i, 0)),
            pl.BlockSpec((tm, D_DELTA), lambda i: (i, z_blk)),
            pl.BlockSpec((1, HEAD_DIM), lambda i: (0, 0)),
            pl.BlockSpec((tm, D_CONV), lambda i: (i, 0)),
            pl.BlockSpec((D_DELTA, D_MODEL), lambda i: (0, 0)),
            pl.BlockSpec((D_CONV, D_MODEL), lambda i: (1, 0)),
            pl.BlockSpec((tm, D_MODEL), lambda i: (i, 0)),
            pl.BlockSpec((8, D_MODEL), lambda i: (0, 2)),
            pl.BlockSpec((1, D_MODEL), lambda i: (0, 0)),
        ],
        out_specs=pl.BlockSpec((tm, D_MODEL), lambda i: (i, 0)),
        out_shape=jax.ShapeDtypeStruct((m, D_MODEL), F32),
        compiler_params=pltpu.CompilerParams(
            dimension_semantics=("arbitrary",), vmem_limit_bytes=VMEM_LIMIT),
        name="out_proj",
    )(o_f, o_b, p2, dnw, mc, w_out, w_out, x2, mod, norm_w)


def _ffn_kernel(x_ref, nw_ref, sh_ref, sc_ref, g_ref, pw_ref, wg_ref, wu_ref, wd_ref, o_ref, h_ref, *,
                tiles_per_batch, n_k):
    i = pl.program_id(0)
    k = pl.program_id(1)
    b = i // tiles_per_batch

    @pl.when(k == 0)
    def _():
        sh = sh_ref[pl.ds(b, 1), :]
        sc = sc_ref[pl.ds(b, 1), :]
        h_ref[...] = (_rms(x_ref[...], nw_ref[...]) * (1.0 + sc) + sh).astype(BF16)
        o_ref[...] = jnp.zeros(o_ref.shape, F32)

    hh = h_ref[...]
    gate = jnp.dot(hh, wg_ref[...], preferred_element_type=F32)
    up = jnp.dot(hh, wu_ref[...], preferred_element_type=F32)
    act = (_silu(gate) * up).astype(BF16)
    o_ref[...] += jnp.dot(act, wd_ref[...], preferred_element_type=F32)

    @pl.when(k == n_k - 1)
    def _():
        g = g_ref[pl.ds(b, 1), :]
        o_ref[...] = x_ref[...] + g * _rms(o_ref[...], pw_ref[...])


def _ffn_call(x2, mod, norm_pre, norm_post, wg, wu, wd, *, tm, tf, tiles_per_batch):
    m = x2.shape[0]
    dff = wg.shape[1]
    n_k = dff // tf
    return pl.pallas_call(
        functools.partial(_ffn_kernel, tiles_per_batch=tiles_per_batch, n_k=n_k),
        grid=(m // tm, n_k),
        in_specs=[
            pl.BlockSpec((tm, D_MODEL), lambda i, k: (i, 0), pipeline_mode=pl.Buffered(1)),
            pl.BlockSpec((1, D_MODEL), lambda i, k: (0, 0)),
            pl.BlockSpec((8, D_MODEL), lambda i, k: (0, 3)),
            pl.BlockSpec((8, D_MODEL), lambda i, k: (0, 4)),
            pl.BlockSpec((8, D_MODEL), lambda i, k: (0, 5)),
            pl.BlockSpec((1, D_MODEL), lambda i, k: (0, 0)),
            pl.BlockSpec((D_MODEL, tf), lambda i, k: (0, k)),
            pl.BlockSpec((D_MODEL, tf), lambda i, k: (0, k)),
            pl.BlockSpec((tf, D_MODEL), lambda i, k: (k, 0)),
        ],
        out_specs=pl.BlockSpec((tm, D_MODEL), lambda i, k: (i, 0)),
        out_shape=jax.ShapeDtypeStruct((m, D_MODEL), F32),
        scratch_shapes=[pltpu.VMEM((tm, D_MODEL), BF16)],
        compiler_params=pltpu.CompilerParams(
            dimension_semantics=("arbitrary", "arbitrary"), vmem_limit_bytes=VMEM_LIMIT),
        name="ffn",
    )(x2, norm_pre, mod, mod, mod, norm_post, wg, wu, wd)


@jax.jit
def _forward(x, c, ctx, c_ctx, w_mod, b_mod, mix_norm_pre, mix_norm_post, w_in, qkv_conv,
             a_log, dt_bias, delta_out_norm, conf_dw, conf_ln_w, conf_ln_b, w_out,
             ffn_norm_pre, ffn_norm_post, w_gate, w_up, w_down):
    bsz, seq, d = x.shape
    ctx_len = ctx.shape[1]
    assert d == D_MODEL and w_mod.shape[0] == 1, "single-layer kernel"
    assert seq % 1024 == 0 and ctx_len % CHUNK == 0 and bsz + 1 <= 8

    cc = jnp.zeros((8, D_MODEL), F32).at[:bsz].set(c).at[bsz].set(c_ctx)
    mod = _mod_call(cc, w_mod[0], b_mod[0][None, :])

    w_in0 = w_in[0]
    w_main = jnp.concatenate([w_in0[:, :V_END].astype(BF16), w_in0[:, STATE_END:].astype(BF16)], axis=1)
    w_g = jnp.zeros((D_MODEL, LANES), BF16).at[:, :N_FEAT].set(w_in0[:, V_END:STATE_END].astype(BF16))
    prm = (jnp.zeros((8, LANES), F32)
           .at[0, 2 * N_HEADS:N_FEAT].set(a_log[0].reshape(-1))
           .at[1, 2 * N_HEADS:N_FEAT].set(dt_bias[0].reshape(-1)))

    x2 = x.reshape(bsz * seq, D_MODEL)
    ctx2 = ctx.reshape(bsz * ctx_len, D_MODEL)
    t = _TILES
    p, ba = _inproj_call(x2, mix_norm_pre, mod, w_main, w_g, n_cols=D_MAIN, tm=t.inproj_tm, tn=t.inproj_tn,
                         tiles_per_mod_row=seq // t.inproj_tm, mod_row0=0)
    pc, bac = _inproj_call(ctx2, mix_norm_pre, mod, w_main, w_g, n_cols=V_END, tm=bsz * ctx_len,
                           tn=t.inproj_tn, tiles_per_mod_row=1, mod_row0=bsz)

    fc, fr = _gatefeat_call(ba.reshape(bsz, seq, LANES), prm)
    fc_c, fr_c = _gatefeat_call(bac.reshape(bsz, ctx_len, LANES), prm)
    qkv = _qkvconv_call(p.reshape(bsz, seq, D_MAIN), qkv_conv[0], tile=t.qkv_tile, heads_per_step=1)
    qkv_c = _qkvconv_call(pc.reshape(bsz, ctx_len, V_END), qkv_conv[0], tile=t.qkv_tile,
                          heads_per_step=N_HEADS)

    zero_state = jnp.zeros((bsz, 2, N_HEADS, HEAD_DIM, HEAD_DIM), F32)
    _, _, s_ctx = _delta_call(qkv_c, fc_c, fr_c, zero_state, lb=ctx_len)
    o_f, o_b, _ = _delta_call(qkv, fc, fr, s_ctx, lb=t.delta_block)

    mix_c = _conformer_call(p, conf_dw[0], conf_ln_w, conf_ln_b, tt=t.conf_rows)

    x1 = _outproj_call(o_f.reshape(bsz * seq, D_DELTA), o_b.reshape(bsz * seq, D_DELTA), p, delta_out_norm,
                       mix_c, w_out[0].astype(BF16), x2, mod, mix_norm_post,
                       tm=t.outproj_tm, tiles_per_batch=seq // t.outproj_tm)
    out = _ffn_call(x1, mod, ffn_norm_pre, ffn_norm_post, w_gate[0].astype(BF16), w_up[0].astype(BF16),
                    w_down[0].astype(BF16), tm=t.ffn_tm, tf=t.ffn_tf, tiles_per_batch=seq // t.ffn_tm)
    return out.reshape(bsz, seq, D_MODEL)


def kernel(x, c, ctx, c_ctx, w_mod, b_mod, mix_norm_pre, mix_norm_post, w_in, qkv_conv, a_log, dt_bias,
           delta_out_norm, conf_dw, conf_ln_w, conf_ln_b, w_out, ffn_norm_pre, ffn_norm_post,
           w_gate, w_up, w_down):
    return _forward(x, c, ctx, c_ctx, w_mod, b_mod, mix_norm_pre, mix_norm_post, w_in, qkv_conv,
                    a_log, dt_bias, delta_out_norm, conf_dw, conf_ln_w, conf_ln_b, w_out,
                    ffn_norm_pre, ffn_norm_post, w_gate, w_up, w_down)
```
